```python
import jax, jax.numpy as jnp
from jax import lax
import numpy as np

D_MODEL = 2048
BATCH = 8
SEQ = 2048
DEPTH = 1

HEAD_DIM = 64
N_Q_HEADS = 16
N_KV_HEADS = 4
GQA_GROUP = N_Q_HEADS // N_KV_HEADS
WINDOW = 128
ATTN_BLOCK = 128
ATTN_WIDTH = N_Q_HEADS * HEAD_DIM
KV_WIDTH = N_KV_HEADS * HEAD_DIM
SGU_GROUPS = 16
SGU_CHUNK = 128
SGU_WIDTH = 1024
SGU_GROUP_DIM = SGU_WIDTH // SGU_GROUPS
MIX_IN_WIDTH = ATTN_WIDTH + 2 * KV_WIDTH + 2 * SGU_WIDTH
N_BRANCHES = 2
D_FF = 5632
EPS = 1e-6
NEG_INF = -1e30

kernel_name = "hybrid_swa_sgu_macaron_block"


def alibi_slopes(n_heads):
    return np.asarray([2.0 ** (-8.0 * (i + 1) / n_heads) for i in range(n_heads)], dtype=np.float32)


def rmsnorm(x, g):
    xf = x.astype(jnp.float32)
    y = xf * lax.rsqrt(jnp.mean(xf * xf, axis=-1, keepdims=True) + EPS)
    return (y * g.astype(jnp.float32)).astype(x.dtype)


def layernorm(x, g, b):
    xf = x.astype(jnp.float32)
    mu = jnp.mean(xf, axis=-1, keepdims=True)
    xc = xf - mu
    y = xc * lax.rsqrt(jnp.mean(xc * xc, axis=-1, keepdims=True) + EPS)
    return (y * g.astype(jnp.float32) + b.astype(jnp.float32)).astype(x.dtype)


def swiglu(x, w_gate, w_up, w_down):
    return (jax.nn.silu(x @ w_gate) * (x @ w_up)) @ w_down


def sliding_window_attention(q, k, v, sinks):
    B, S = q.shape[0], q.shape[1]
    nb = S // ATTN_BLOCK
    qb = q.reshape(B, nb, ATTN_BLOCK, N_KV_HEADS, GQA_GROUP, HEAD_DIM)

    def band(t):
        prev = jnp.pad(t, ((0, 0), (ATTN_BLOCK, 0), (0, 0), (0, 0)))[:, :S]
        prev = prev.reshape(B, nb, ATTN_BLOCK, N_KV_HEADS, HEAD_DIM)
        cur = t.reshape(B, nb, ATTN_BLOCK, N_KV_HEADS, HEAD_DIM)
        return jnp.concatenate([prev, cur], axis=2)

    kb, vb = band(k), band(v)
    scores = jnp.einsum('bnqhgd,bnkhd->bnhgqk', qb, kb).astype(jnp.float32) * (HEAD_DIM ** -0.5)

    qpos = jnp.arange(ATTN_BLOCK)[:, None] + ATTN_BLOCK
    kpos = jnp.arange(2 * ATTN_BLOCK)[None, :]
    dist = (qpos - kpos).astype(jnp.float32)
    blk = jnp.arange(nb)[:, None, None]
    valid = (dist >= 0) & (dist < WINDOW) & (blk * ATTN_BLOCK + kpos[None] - ATTN_BLOCK >= 0)

    slopes = jnp.asarray(alibi_slopes(N_Q_HEADS)).reshape(N_KV_HEADS, GQA_GROUP)
    scores = scores - slopes[None, None, :, :, None, None] * jnp.abs(dist)[None, None, None, None]
    scores = jnp.where(valid[None, :, None, None], scores, NEG_INF)

    sink = sinks.astype(jnp.float32).reshape(N_KV_HEADS, GQA_GROUP)
    sink = jnp.broadcast_to(sink[None, None, :, :, None, None], scores.shape[:-1] + (1,))
    probs = jax.nn.softmax(jnp.concatenate([scores, sink], axis=-1), axis=-1)[..., :-1]
    out = jnp.einsum('bnhgqk,bnkhd->bnqhgd', probs.astype(v.dtype), vb)
    return out.reshape(B, S, ATTN_WIDTH)


def chunked_spatial_gating(u, z, ln_g, ln_b, w_s, b_s):
    B, S = u.shape[0], u.shape[1]
    nc = S // SGU_CHUNK
    z = layernorm(z, ln_g, ln_b).reshape(B, nc, SGU_CHUNK, SGU_GROUPS, SGU_GROUP_DIM)
    causal = jnp.tril(jnp.ones((SGU_CHUNK, SGU_CHUNK), dtype=bool))
    ws = jnp.where(causal[None], w_s, 0.0).astype(z.dtype)
    mixed = jnp.einsum('gts,bnsgc->bntgc', ws, z) + b_s.T[None, None, :, :, None].astype(z.dtype)
    return u * mixed.reshape(B, S, SGU_WIDTH)


def setup_inputs(seed: int = 0) -> dict:
    key = jax.random.key(seed)
    ks = jax.random.split(key, 24)
    L, D = DEPTH, D_MODEL

    def w(k, shape, fan_in, scale=1.0):
        return jax.random.normal(k, shape, dtype=jnp.float32) * (scale * fan_in ** -0.5)

    def gain(k, shape):
        return 1.0 + 0.05 * jax.random.normal(k, shape, dtype=jnp.float32)

    return {
        "x": jax.random.normal(ks[0], (BATCH, SEQ, D), dtype=jnp.float32),
        "ffn1_norm": gain(ks[1], (L, D)),
        "ffn1_w_gate": w(ks[2], (L, D, D_FF), D),
        "ffn1_w_up": w(ks[3], (L, D, D_FF), D),
        "ffn1_w_down": w(ks[4], (L, D_FF, D), D_FF),
        "mix_norm": gain(ks[5], (L, D)),
        "w_in": w(ks[6], (L, D, MIX_IN_WIDTH), D),
        "attn_sinks": 0.5 * jax.random.normal(ks[7], (L, N_Q_HEADS), dtype=jnp.float32),
        "sgu_norm_g": gain(ks[8], (L, SGU_WIDTH)),
        "sgu_norm_b": 0.02 * jax.random.normal(ks[9], (L, SGU_WIDTH), dtype=jnp.float32),
        "sgu_w_s": w(ks[10], (L, SGU_GROUPS, SGU_CHUNK, SGU_CHUNK), SGU_CHUNK, 0.5),
        "sgu_b_s": 1.0 + 0.1 * jax.random.normal(ks[11], (L, SGU_GROUPS, SGU_CHUNK), dtype=jnp.float32),
        "w_proj_attn": w(ks[12], (L, ATTN_WIDTH, D), ATTN_WIDTH),
        "w_proj_sgu": w(ks[13], (L, SGU_WIDTH, D), SGU_WIDTH),
        "w_branch_gate": w(ks[14], (L, D, N_BRANCHES * D), D),
        "b_branch_gate": 0.02 * jax.random.normal(ks[15], (L, N_BRANCHES * D), dtype=jnp.float32),
        "w_out": w(ks[16], (L, D, D), D),
        "ffn2_norm": gain(ks[17], (L, D)),
        "ffn2_w_gate": w(ks[18], (L, D, D_FF), D),
        "ffn2_w_up": w(ks[19], (L, D, D_FF), D),
        "ffn2_w_down": w(ks[20], (L, D_FF, D), D_FF),
        "final_norm": gain(ks[21], (D,)),
    }


def reference(x, ffn1_norm, ffn1_w_gate, ffn1_w_up, ffn1_w_down, mix_norm, w_in, attn_sinks,
              sgu_norm_g, sgu_norm_b, sgu_w_s, sgu_b_s, w_proj_attn, w_proj_sgu,
              w_branch_gate, b_branch_gate, w_out, ffn2_norm, ffn2_w_gate, ffn2_w_up,
              ffn2_w_down, final_norm):
    B, S = x.shape[0], x.shape[1]
    splits = np.cumsum([ATTN_WIDTH, KV_WIDTH, KV_WIDTH, SGU_WIDTH]).tolist()
    for l in range(DEPTH):
        x = x + 0.5 * swiglu(rmsnorm(x, ffn1_norm[l]), ffn1_w_gate[l], ffn1_w_up[l], ffn1_w_down[l])

        h = rmsnorm(x, mix_norm[l])
        proj = h @ w_in[l]
        q, k, v, u, z = jnp.split(proj, splits, axis=-1)
        attn = sliding_window_attention(
            q.reshape(B, S, N_Q_HEADS, HEAD_DIM),
            k.reshape(B, S, N_KV_HEADS, HEAD_DIM),
            v.reshape(B, S, N_KV_HEADS, HEAD_DIM),
            attn_sinks[l])
        sgu = chunked_spatial_gating(jax.nn.gelu(u), jax.nn.gelu(z), sgu_norm_g[l], sgu_norm_b[l],
                                     sgu_w_s[l], sgu_b_s[l])
        gates = jax.nn.sigmoid(h @ w_branch_gate[l] + b_branch_gate[l])
        g_attn, g_sgu = jnp.split(gates, N_BRANCHES, axis=-1)
        merged = g_attn * (attn @ w_proj_attn[l]) + g_sgu * (sgu @ w_proj_sgu[l])
        x = x + merged @ w_out[l]

        x = x + 0.5 * swiglu(rmsnorm(x, ffn2_norm[l]), ffn2_w_gate[l], ffn2_w_up[l], ffn2_w_down[l])
    return rmsnorm(x, final_norm)
```

```python
import functools
import math

import jax
import jax.numpy as jnp
from jax import lax
from jax.experimental import pallas as pl
from jax.experimental.pallas import tpu as pltpu

D_MODEL = 2048
SEQ = 2048
HEAD_DIM = 64
N_Q_HEADS = 16
N_KV_HEADS = 4
GQA_GROUP = N_Q_HEADS // N_KV_HEADS
BLOCK = 128
ATTN_WIDTH = N_Q_HEADS * HEAD_DIM
KV_WIDTH = N_KV_HEADS * HEAD_DIM
SGU_GROUPS = 16
SGU_WIDTH = 1024
SGU_GROUP_DIM = SGU_WIDTH // SGU_GROUPS
D_FF = 5632
EPS = 1e-6
NEG_INF = -1e30

BF16 = jnp.bfloat16
F32 = jnp.float32

FFN_BM = 1024
FFN_BF = 512
PROJ_BM = 1024
PROJ_BN = 512
MIX_BM = 1024
MERGE_BM = 512
MERGE_BN = 512
NORM_ROWS = 128
VMEM_LIMIT = 56 * 1024 * 1024

ALIBI_SLOPES = tuple(2.0 ** (-8.0 * (i + 1) / N_Q_HEADS) for i in range(N_Q_HEADS))


def _rmsnorm_rows(x_ref, g_ref, out_ref, n_rows):
    def body(r, carry):
        rows = pl.ds(pl.multiple_of(r * NORM_ROWS, NORM_ROWS), NORM_ROWS)
        xs = x_ref[rows, :]
        ms = jnp.mean(xs * xs, axis=-1, keepdims=True)
        out_ref[rows, :] = (xs * lax.rsqrt(ms + EPS) * g_ref[...]).astype(out_ref.dtype)
        return carry
    lax.fori_loop(0, n_rows // NORM_ROWS, body, 0)


def _ffn_kernel(*refs, final_norm):
    if final_norm:
        x_ref, g_ref, wg_ref, wu_ref, wd_ref, fg_ref, o_ref, h_ref = refs
    else:
        x_ref, g_ref, wg_ref, wu_ref, wd_ref, o_ref, h_ref = refs
    j = pl.program_id(1)

    @pl.when(j == 0)
    def _():
        _rmsnorm_rows(x_ref, g_ref, h_ref, FFN_BM)
        o_ref[...] = jnp.zeros_like(o_ref)

    h = h_ref[...]
    gate = jnp.dot(h, wg_ref[...], preferred_element_type=F32)
    up = jnp.dot(h, wu_ref[...], preferred_element_type=F32)
    act = (gate * jax.nn.sigmoid(gate) * up).astype(BF16)
    o_ref[...] += jnp.dot(act, wd_ref[...], preferred_element_type=F32)

    @pl.when(j == pl.num_programs(1) - 1)
    def _():
        def body(r, carry):
            rows = pl.ds(pl.multiple_of(r * NORM_ROWS, NORM_ROWS), NORM_ROWS)
            y = x_ref[rows, :] + 0.5 * o_ref[rows, :]
            if final_norm:
                ms = jnp.mean(y * y, axis=-1, keepdims=True)
                y = y * lax.rsqrt(ms + EPS) * fg_ref[...]
            o_ref[rows, :] = y
            return carry
        lax.fori_loop(0, FFN_BM // NORM_ROWS, body, 0)


def _ffn(x, norm_g, w_gate, w_up, w_down, final_g=None):
    n = x.shape[0]
    final_norm = final_g is not None
    in_specs = [
        pl.BlockSpec((FFN_BM, D_MODEL), lambda i, j: (i, 0), pipeline_mode=pl.Buffered(1)),
        pl.BlockSpec((1, D_MODEL), lambda i, j: (0, 0)),
        pl.BlockSpec((D_MODEL, FFN_BF), lambda i, j: (0, j)),
        pl.BlockSpec((D_MODEL, FFN_BF), lambda i, j: (0, j)),
        pl.BlockSpec((FFN_BF, D_MODEL), lambda i, j: (j, 0)),
    ]
    args = [x, norm_g, w_gate, w_up, w_down]
    if final_norm:
        in_specs.append(pl.BlockSpec((1, D_MODEL), lambda i, j: (0, 0)))
        args.append(final_g)
    return pl.pallas_call(
        functools.partial(_ffn_kernel, final_norm=final_norm),
        out_shape=jax.ShapeDtypeStruct((n, D_MODEL), F32),
        grid=(n // FFN_BM, D_FF // FFN_BF),
        in_specs=in_specs,
        out_specs=pl.BlockSpec((FFN_BM, D_MODEL), lambda i, j: (i, 0)),
        scratch_shapes=[pltpu.VMEM((FFN_BM, D_MODEL), BF16)],
        compiler_params=pltpu.CompilerParams(
            dimension_semantics=("arbitrary", "arbitrary"),
            vmem_limit_bytes=VMEM_LIMIT),
        name="ffn_final" if final_norm else "ffn",
    )(*args)


N_Q_CHUNKS = ATTN_WIDTH // PROJ_BN
N_QKV_CHUNKS = (ATTN_WIDTH + 2 * KV_WIDTH) // PROJ_BN
N_U_CHUNKS = SGU_WIDTH // PROJ_BN
U_START = N_QKV_CHUNKS
Z_START = U_START + N_U_CHUNKS
G_START = Z_START + N_U_CHUNKS
N_G_CHUNKS = 2 * D_MODEL // PROJ_BN
N_PROJ_CHUNKS = G_START + N_G_CHUNKS


def _gelu_tanh(x):
    c = math.sqrt(2.0 / math.pi)
    return 0.5 * x * (1.0 + jnp.tanh(c * (x + 0.044715 * (x * x * x))))


def _proj_kernel(x_ref, g_ref, w_ref, b_ref, qkv_ref, u_ref, z_ref, gt_ref, h_ref):
    j = pl.program_id(1)

    @pl.when(j == 0)
    def _():
        _rmsnorm_rows(x_ref, g_ref, h_ref, PROJ_BM)

    acc = jnp.dot(h_ref[...], w_ref[...], preferred_element_type=F32)

    @pl.when(j < N_Q_CHUNKS)
    def _():
        qkv_ref[...] = (acc * (HEAD_DIM ** -0.5)).astype(BF16)

    @pl.when((j >= N_Q_CHUNKS) & (j < U_START))
    def _():
        qkv_ref[...] = acc.astype(BF16)

    @pl.when((j >= U_START) & (j < Z_START))
    def _():
        u_ref[...] = _gelu_tanh(acc).astype(BF16)

    @pl.when((j >= Z_START) & (j < G_START))
    def _():
        z_ref[...] = _gelu_tanh(acc).astype(BF16)

    @pl.when(j >= G_START)
    def _():
        gt_ref[...] = jax.nn.sigmoid(acc + b_ref[...]).astype(BF16)


def _proj(x1, norm_g, w_cat, b_gate):
    n = x1.shape[0]
    return pl.pallas_call(
        _proj_kernel,
        out_shape=(
            jax.ShapeDtypeStruct((n, ATTN_WIDTH + 2 * KV_WIDTH), BF16),
            jax.ShapeDtypeStruct((n, SGU_WIDTH), BF16),
            jax.ShapeDtypeStruct((n, SGU_WIDTH), BF16),
            jax.ShapeDtypeStruct((n, 2 * D_MODEL), BF16),
        ),
        grid=(n // PROJ_BM, N_PROJ_CHUNKS),
        in_specs=[
            pl.BlockSpec((PROJ_BM, D_MODEL), lambda i, j: (i, 0)),
            pl.BlockSpec((1, D_MODEL), lambda i, j: (0, 0)),
            pl.BlockSpec((D_MODEL, PROJ_BN), lambda i, j: (0, j)),
            pl.BlockSpec((1, PROJ_BN), lambda i, j: (0, jnp.maximum(j - G_START, 0))),
        ],
        out_specs=(
            pl.BlockSpec((PROJ_BM, PROJ_BN), lambda i, j: (i, jnp.minimum(j, N_QKV_CHUNKS - 1))),
            pl.BlockSpec((PROJ_BM, PROJ_BN), lambda i, j: (i, jnp.clip(j - U_START, 0, N_U_CHUNKS - 1))),
            pl.BlockSpec((PROJ_BM, PROJ_BN), lambda i, j: (i, jnp.clip(j - Z_START, 0, N_U_CHUNKS - 1))),
            pl.BlockSpec((PROJ_BM, PROJ_BN), lambda i, j: (i, jnp.maximum(j - G_START, 0))),
        ),
        scratch_shapes=[pltpu.VMEM((PROJ_BM, D_MODEL), BF16)],
        compiler_params=pltpu.CompilerParams(
            dimension_semantics=("arbitrary", "arbitrary"),
            vmem_limit_bytes=VMEM_LIMIT),
        name="proj",
    )(x1, norm_g, w_cat, b_gate)


MIX_SUB = MIX_BM // BLOCK
K_OFF = ATTN_WIDTH
V_OFF = ATTN_WIDTH + KV_WIDTH


def _mixer_kernel(sinks_ref, qkv_ref, halo_ref, u_ref, z_ref, ws_ref, bs_ref, lng_ref, lnb_ref,
                  attn_ref, sgu_ref, kv_buf, ws_buf):
    i = pl.program_id(0)

    kv_buf[0:BLOCK, :] = halo_ref[...]
    kv_buf[BLOCK:, :] = qkv_ref[:, K_OFF:]

    tq = lax.broadcasted_iota(jnp.int32, (BLOCK, BLOCK), 0)
    ts = lax.broadcasted_iota(jnp.int32, (BLOCK, BLOCK), 1)
    causal = ts <= tq
    for g in range(SGU_GROUPS):
        ws_buf[g] = jnp.where(causal, ws_ref[g], 0.0).astype(BF16)

    qpos = lax.broadcasted_iota(jnp.int32, (BLOCK, 2 * BLOCK), 0) + BLOCK
    kpos = lax.broadcasted_iota(jnp.int32, (BLOCK, 2 * BLOCK), 1)
    dist_i = qpos - kpos
    in_window = (dist_i >= 0) & (dist_i < BLOCK)
    dist = jnp.abs(dist_i).astype(F32)
    step_starts_sequence = (i % (SEQ // MIX_BM)) == 0

    def block_body(r, carry):
        row0 = pl.multiple_of(r * BLOCK, BLOCK)
        rows = pl.ds(row0, BLOCK)
        band = pl.ds(row0, 2 * BLOCK)

        has_prev = jnp.logical_not(jnp.logical_and(step_starts_sequence, r == 0))
        valid = in_window & ((kpos >= BLOCK) | has_prev)

        outs = []
        for hk in range(N_KV_HEADS):
            k_band = kv_buf[band, hk * HEAD_DIM:(hk + 1) * HEAD_DIM]
            v_band = kv_buf[band, KV_WIDTH + hk * HEAD_DIM:KV_WIDTH + (hk + 1) * HEAD_DIM]
            for gq in range(GQA_GROUP):
                h = hk * GQA_GROUP + gq
                q_h = qkv_ref[rows, h * HEAD_DIM:(h + 1) * HEAD_DIM]
                s = lax.dot_general(q_h, k_band, (((1,), (1,)), ((), ())),
                                    preferred_element_type=F32)
                s = jnp.where(valid, s - ALIBI_SLOPES[h] * dist, NEG_INF)
                sink = sinks_ref[h]
                m = jnp.maximum(jnp.max(s, axis=-1, keepdims=True), sink)
                p = jnp.exp(s - m)
                denom = jnp.sum(p, axis=-1, keepdims=True) + jnp.exp(sink - m)
                o = jnp.dot(p.astype(BF16), v_band, preferred_element_type=F32)
                outs.append(o / denom)
        attn_ref[rows, :] = jnp.concatenate(outs, axis=-1).astype(BF16)

        zc = z_ref[rows, :].astype(F32)
        mu = jnp.mean(zc, axis=-1, keepdims=True)
        zc = zc - mu
        var = jnp.mean(zc * zc, axis=-1, keepdims=True)
        zl = (zc * lax.rsqrt(var + EPS) * lng_ref[...] + lnb_ref[...]).astype(BF16)
        mixed = [
            jnp.dot(ws_buf[g], zl[:, g * SGU_GROUP_DIM:(g + 1) * SGU_GROUP_DIM],
                    preferred_element_type=F32)
            for g in range(SGU_GROUPS)
        ]
        mixed = jnp.concatenate(mixed, axis=-1) + bs_ref[...]
        sgu_ref[rows, :] = (u_ref[rows, :].astype(F32) * mixed).astype(BF16)
        return carry

    lax.fori_loop(0, MIX_SUB, block_body, 0)


def _mixer(qkv, u, z, sinks, ws, bs_full, ln_g, ln_b):
    n = qkv.shape[0]
    qkv_w = ATTN_WIDTH + 2 * KV_WIDTH
    return pl.pallas_call(
        _mixer_kernel,
        out_shape=(
            jax.ShapeDtypeStruct((n, ATTN_WIDTH), BF16),
            jax.ShapeDtypeStruct((n, SGU_WIDTH), BF16),
        ),
        grid=(n // MIX_BM,),
        in_specs=[
            pl.BlockSpec(memory_space=pltpu.SMEM),
            pl.BlockSpec((MIX_BM, qkv_w), lambda i: (i, 0)),
            pl.BlockSpec((BLOCK, 2 * KV_WIDTH),
                         lambda i: (jnp.maximum(i * MIX_SUB - 1, 0), ATTN_WIDTH // (2 * KV_WIDTH))),
            pl.BlockSpec((MIX_BM, SGU_WIDTH), lambda i: (i, 0)),
            pl.BlockSpec((MIX_BM, SGU_WIDTH), lambda i: (i, 0)),
            pl.BlockSpec((SGU_GROUPS, BLOCK, BLOCK), lambda i: (0, 0, 0)),
            pl.BlockSpec((BLOCK, SGU_WIDTH), lambda i: (0, 0)),
            pl.BlockSpec((1, SGU_WIDTH), lambda i: (0, 0)),
            pl.BlockSpec((1, SGU_WIDTH), lambda i: (0, 0)),
        ],
        out_specs=(
            pl.BlockSpec((MIX_BM, ATTN_WIDTH), lambda i: (i, 0)),
            pl.BlockSpec((MIX_BM, SGU_WIDTH), lambda i: (i, 0)),
        ),
        scratch_shapes=[
            pltpu.VMEM((MIX_BM + BLOCK, 2 * KV_WIDTH), BF16),
            pltpu.VMEM((SGU_GROUPS, BLOCK, BLOCK), BF16),
        ],
        compiler_params=pltpu.CompilerParams(
            dimension_semantics=("arbitrary",),
            vmem_limit_bytes=VMEM_LIMIT),
        name="mixer",
    )(sinks, qkv, qkv, u, z, ws, bs_full, ln_g, ln_b)


def _merge_kernel(attn_ref, sgu_ref, gt_ref, x_ref, pa_ref, pb_ref, wo_ref, o_ref, m_ref):
    attn = attn_ref[...]
    sgu = sgu_ref[...]
    for c in range(D_MODEL // MERGE_BN):
        cols = slice(c * MERGE_BN, (c + 1) * MERGE_BN)
        a = jnp.dot(attn, pa_ref[:, cols], preferred_element_type=F32)
        b = jnp.dot(sgu, pb_ref[:, cols], preferred_element_type=F32)
        g_a = gt_ref[:, cols].astype(F32)
        g_b = gt_ref[:, D_MODEL + c * MERGE_BN:D_MODEL + (c + 1) * MERGE_BN].astype(F32)
        m_ref[:, cols] = (g_a * a + g_b * b).astype(BF16)
    o_ref[...] = x_ref[...] + jnp.dot(m_ref[...], wo_ref[...], preferred_element_type=F32)


def _merge(attn, sgu, gates, x1, p_a, p_b, w_out):
    n = x1.shape[0]
    const = lambda i: (0, 0)
    return pl.pallas_call(
        _merge_kernel,
        out_shape=jax.ShapeDtypeStruct((n, D_MODEL), F32),
        grid=(n // MERGE_BM,),
        in_specs=[
            pl.BlockSpec((MERGE_BM, ATTN_WIDTH), lambda i: (i, 0)),
            pl.BlockSpec((MERGE_BM, SGU_WIDTH), lambda i: (i, 0)),
            pl.BlockSpec((MERGE_BM, 2 * D_MODEL), lambda i: (i, 0)),
            pl.BlockSpec((MERGE_BM, D_MODEL), lambda i: (i, 0)),
            pl.BlockSpec((ATTN_WIDTH, D_MODEL), const, pipeline_mode=pl.Buffered(1)),
            pl.BlockSpec((SGU_WIDTH, D_MODEL), const, pipeline_mode=pl.Buffered(1)),
            pl.BlockSpec((D_MODEL, D_MODEL), const, pipeline_mode=pl.Buffered(1)),
        ],
        out_specs=pl.BlockSpec((MERGE_BM, D_MODEL), lambda i: (i, 0)),
        scratch_shapes=[pltpu.VMEM((MERGE_BM, D_MODEL), BF16)],
        compiler_params=pltpu.CompilerParams(
            dimension_semantics=("arbitrary",),
            vmem_limit_bytes=VMEM_LIMIT),
        name="merge",
    )(attn, sgu, gates, x1, p_a, p_b, w_out)


def kernel(x, ffn1_norm, ffn1_w_gate, ffn1_w_up, ffn1_w_down, mix_norm, w_in, attn_sinks,
           sgu_norm_g, sgu_norm_b, sgu_w_s, sgu_b_s, w_proj_attn, w_proj_sgu,
           w_branch_gate, b_branch_gate, w_out, ffn2_norm, ffn2_w_gate, ffn2_w_up,
           ffn2_w_down, final_norm):
    batch, seq, d = x.shape
    assert (seq, d) == (SEQ, D_MODEL) and ffn1_norm.shape[0] == 1
    n = batch * seq
    xf = x.reshape(n, d)
    bf = lambda w: w.astype(BF16)
    row = lambda v: v.reshape(1, -1)

    x1 = _ffn(xf, row(ffn1_norm[0]), bf(ffn1_w_gate[0]), bf(ffn1_w_up[0]), bf(ffn1_w_down[0]))

    w_cat = jnp.concatenate([bf(w_in[0]), bf(w_branch_gate[0])], axis=1)
    qkv, u, z, gates = _proj(x1, row(mix_norm[0]), w_cat, row(b_branch_gate[0]))

    bs_full = jnp.repeat(sgu_b_s[0].T, SGU_GROUP_DIM, axis=1)
    attn, sgu = _mixer(qkv, u, z, attn_sinks[0], sgu_w_s[0], bs_full,
                       row(sgu_norm_g[0]), row(sgu_norm_b[0]))

    x2 = _merge(attn, sgu, gates, x1, bf(w_proj_attn[0]), bf(w_proj_sgu[0]), bf(w_out[0]))

    out = _ffn(x2, row(ffn2_norm[0]), bf(ffn2_w_gate[0]), bf(ffn2_w_up[0]), bf(ffn2_w_down[0]),
               final_g=row(final_norm))
    return out.reshape(batch, seq, d)
```

```python
import functools
import math

import jax
import jax.numpy as jnp
from jax import lax
from jax.experimental import pallas as pl
from jax.experimental.pallas import tpu as pltpu

D_MODEL = 2048
SEQ = 2048
HEAD_DIM = 64
N_Q_HEADS = 16
N_KV_HEADS = 4
GQA_GROUP = N_Q_HEADS // N_KV_HEADS
BLOCK = 128
ATTN_WIDTH = N_Q_HEADS * HEAD_DIM
KV_WIDTH = N_KV_HEADS * HEAD_DIM
QKV_WIDTH = ATTN_WIDTH + 2 * KV_WIDTH
SGU_GROUPS = 16
SGU_WIDTH = 1024
SGU_GROUP_DIM = SGU_WIDTH // SGU_GROUPS
D_FF = 5632
EPS = 1e-6
NEG_INF = -1e30

BF16 = jnp.bfloat16
F32 = jnp.float32

FFN_BM = 1024
FFN_BF = 512
PROJ_BM = 1024
PROJ_BN = 512
MIX_BM = 1024
MERGE_BM = 512
MERGE_BN = 512
NORM_ROWS = 128
VMEM_LIMIT = 56 * 1024 * 1024

ALIBI_SLOPES = tuple(2.0 ** (-8.0 * (i + 1) / N_Q_HEADS) for i in range(N_Q_HEADS))


def _rms_scale(y, g):
    ms = jnp.mean(y * y, axis=-1, keepdims=True)
    return y * lax.rsqrt(ms + EPS) * g


def _ffn_kernel(*refs, final_norm):
    if final_norm:
        x_ref, g_ref, wg_ref, wu_ref, wd_ref, ng_ref, o_ref, h_ref = refs
    else:
        x_ref, g_ref, wg_ref, wu_ref, wd_ref, ng_ref, o_ref, hn_ref, h_ref = refs
    j = pl.program_id(1)

    @pl.when(j == 0)
    def _():
        def body(r, carry):
            rows = pl.ds(pl.multiple_of(r * NORM_ROWS, NORM_ROWS), NORM_ROWS)
            h_ref[rows, :] = _rms_scale(x_ref[rows, :], g_ref[...]).astype(BF16)
            return carry
        lax.fori_loop(0, FFN_BM // NORM_ROWS, body, 0)
        o_ref[...] = jnp.zeros_like(o_ref)

    h = h_ref[...]
    gate = jnp.dot(h, wg_ref[...], preferred_element_type=F32)
    up = jnp.dot(h, wu_ref[...], preferred_element_type=F32)
    act = (gate * jax.nn.sigmoid(gate) * up).astype(BF16)
    o_ref[...] += jnp.dot(act, wd_ref[...], preferred_element_type=F32)

    @pl.when(j == pl.num_programs(1) - 1)
    def _():
        def body(r, carry):
            rows = pl.ds(pl.multiple_of(r * NORM_ROWS, NORM_ROWS), NORM_ROWS)
            y = x_ref[rows, :] + 0.5 * o_ref[rows, :]
            normed = _rms_scale(y, ng_ref[...])
            if final_norm:
                o_ref[rows, :] = normed
            else:
                o_ref[rows, :] = y
                hn_ref[rows, :] = normed.astype(BF16)
            return carry
        lax.fori_loop(0, FFN_BM // NORM_ROWS, body, 0)


def _ffn(x, norm_g, w_gate, w_up, w_down, next_g, *, final_norm):
    n = x.shape[0]
    row_block = lambda i, j: (i, 0)
    const = lambda i, j: (0, 0)
    out_shape = [jax.ShapeDtypeStruct((n, D_MODEL), F32)]
    out_specs = [pl.BlockSpec((FFN_BM, D_MODEL), row_block)]
    if not final_norm:
        out_shape.append(jax.ShapeDtypeStruct((n, D_MODEL), BF16))
        out_specs.append(pl.BlockSpec((FFN_BM, D_MODEL), row_block))
    return pl.pallas_call(
        functools.partial(_ffn_kernel, final_norm=final_norm),
        out_shape=tuple(out_shape),
        grid=(n // FFN_BM, D_FF // FFN_BF),
        in_specs=[
            pl.BlockSpec((FFN_BM, D_MODEL), row_block, pipeline_mode=pl.Buffered(1)),
            pl.BlockSpec((1, D_MODEL), const),
            pl.BlockSpec((D_MODEL, FFN_BF), lambda i, j: (0, j)),
            pl.BlockSpec((D_MODEL, FFN_BF), lambda i, j: (0, j)),
            pl.BlockSpec((FFN_BF, D_MODEL), lambda i, j: (j, 0)),
            pl.BlockSpec((1, D_MODEL), const),
        ],
        out_specs=tuple(out_specs),
        scratch_shapes=[pltpu.VMEM((FFN_BM, D_MODEL), BF16)],
        compiler_params=pltpu.CompilerParams(
            dimension_semantics=("arbitrary", "arbitrary"),
            vmem_limit_bytes=VMEM_LIMIT),
        name="ffn_final" if final_norm else "ffn",
    )(x, norm_g, w_gate, w_up, w_down, next_g)


def _gelu_tanh(x):
    c = math.sqrt(2.0 / math.pi)
    return 0.5 * x * (1.0 + jnp.tanh(c * (x + 0.044715 * (x * x * x))))


def _proj_kernel(h_ref, wqkv_ref, wuz_ref, qkv_ref, u_ref, z_ref):
    h = h_ref[...]
    for c in range(QKV_WIDTH // PROJ_BN):
        cols = slice(c * PROJ_BN, (c + 1) * PROJ_BN)
        acc = jnp.dot(h, wqkv_ref[:, cols], preferred_element_type=F32)
        if (c + 1) * PROJ_BN <= ATTN_WIDTH:
            acc = acc * (HEAD_DIM ** -0.5)
        qkv_ref[:, cols] = acc.astype(BF16)
    for c in range(2 * SGU_WIDTH // PROJ_BN):
        acc = jnp.dot(h, wuz_ref[:, c * PROJ_BN:(c + 1) * PROJ_BN], preferred_element_type=F32)
        dst = u_ref if c * PROJ_BN < SGU_WIDTH else z_ref
        off = (c * PROJ_BN) % SGU_WIDTH
        dst[:, off:off + PROJ_BN] = _gelu_tanh(acc).astype(BF16)


def _proj(h2, w_qkv, w_uz):
    n = h2.shape[0]
    row_block = lambda i: (i, 0)
    const = lambda i: (0, 0)
    return pl.pallas_call(
        _proj_kernel,
        out_shape=(
            jax.ShapeDtypeStruct((n, QKV_WIDTH), BF16),
            jax.ShapeDtypeStruct((n, SGU_WIDTH), BF16),
            jax.ShapeDtypeStruct((n, SGU_WIDTH), BF16),
        ),
        grid=(n // PROJ_BM,),
        in_specs=[
            pl.BlockSpec((PROJ_BM, D_MODEL), row_block),
            pl.BlockSpec((D_MODEL, QKV_WIDTH), const, pipeline_mode=pl.Buffered(1)),
            pl.BlockSpec((D_MODEL, 2 * SGU_WIDTH), const, pipeline_mode=pl.Buffered(1)),
        ],
        out_specs=(
            pl.BlockSpec((PROJ_BM, QKV_WIDTH), row_block),
            pl.BlockSpec((PROJ_BM, SGU_WIDTH), row_block),
            pl.BlockSpec((PROJ_BM, SGU_WIDTH), row_block),
        ),
        compiler_params=pltpu.CompilerParams(
            dimension_semantics=("arbitrary",),
            vmem_limit_bytes=VMEM_LIMIT),
        name="proj",
    )(h2, w_qkv, w_uz)


def _gates_kernel(h_ref, w_ref, b_ref, gt_ref):
    h = h_ref[...]
    for c in range(2 * D_MODEL // PROJ_BN):
        cols = slice(c * PROJ_BN, (c + 1) * PROJ_BN)
        acc = jnp.dot(h, w_ref[:, cols], preferred_element_type=F32)
        gt_ref[:, cols] = jax.nn.sigmoid(acc + b_ref[:, cols]).astype(BF16)


def _gates(h2, w_gate, b_gate):
    n = h2.shape[0]
    const = lambda i: (0, 0)
    return pl.pallas_call(
        _gates_kernel,
        out_shape=jax.ShapeDtypeStruct((n, 2 * D_MODEL), BF16),
        grid=(n // PROJ_BM,),
        in_specs=[
            pl.BlockSpec((PROJ_BM, D_MODEL), lambda i: (i, 0)),
            pl.BlockSpec((D_MODEL, 2 * D_MODEL), const, pipeline_mode=pl.Buffered(1)),
            pl.BlockSpec((1, 2 * D_MODEL), const),
        ],
        out_specs=pl.BlockSpec((PROJ_BM, 2 * D_MODEL), lambda i: (i, 0)),
        compiler_params=pltpu.CompilerParams(
            dimension_semantics=("arbitrary",),
            vmem_limit_bytes=VMEM_LIMIT),
        name="gates",
    )(h2, w_gate, b_gate)


MIX_SUB = MIX_BM // BLOCK
K_OFF = ATTN_WIDTH
V_OFF = ATTN_WIDTH + KV_WIDTH


def _mixer_kernel(sinks_ref, qkv_ref, halo_ref, u_ref, z_ref, ws_ref, bs_ref, lng_ref, lnb_ref,
                  attn_ref, sgu_ref, kv_buf, ws_buf):
    i = pl.program_id(0)

    kv_buf[0:BLOCK, :] = halo_ref[...]
    kv_buf[BLOCK:, :] = qkv_ref[:, K_OFF:]

    tq = lax.broadcasted_iota(jnp.int32, (BLOCK, BLOCK), 0)
    ts = lax.broadcasted_iota(jnp.int32, (BLOCK, BLOCK), 1)
    causal = ts <= tq
    for g in range(SGU_GROUPS):
        ws_buf[g] = jnp.where(causal, ws_ref[g], 0.0).astype(BF16)

    qpos = lax.broadcasted_iota(jnp.int32, (BLOCK, 2 * BLOCK), 0) + BLOCK
    kpos = lax.broadcasted_iota(jnp.int32, (BLOCK, 2 * BLOCK), 1)
    dist_i = qpos - kpos
    in_window = (dist_i >= 0) & (dist_i < BLOCK)
    dist = jnp.abs(dist_i).astype(F32)
    step_starts_sequence = (i % (SEQ // MIX_BM)) == 0

    def block_body(r, carry):
        row0 = pl.multiple_of(r * BLOCK, BLOCK)
        rows = pl.ds(row0, BLOCK)
        band = pl.ds(row0, 2 * BLOCK)

        has_prev = jnp.logical_not(jnp.logical_and(step_starts_sequence, r == 0))
        valid = in_window & ((kpos >= BLOCK) | has_prev)

        outs = []
        for hk in range(N_KV_HEADS):
            k_band = kv_buf[band, hk * HEAD_DIM:(hk + 1) * HEAD_DIM]
            v_band = kv_buf[band, KV_WIDTH + hk * HEAD_DIM:KV_WIDTH + (hk + 1) * HEAD_DIM]
            for gq in range(GQA_GROUP):
                h = hk * GQA_GROUP + gq
                q_h = qkv_ref[rows, h * HEAD_DIM:(h + 1) * HEAD_DIM]
                s = lax.dot_general(q_h, k_band, (((1,), (1,)), ((), ())),
                                    preferred_element_type=F32)
                s = jnp.where(valid, s - ALIBI_SLOPES[h] * dist, NEG_INF)
                sink = sinks_ref[h]
                m = jnp.maximum(jnp.max(s, axis=-1, keepdims=True), sink)
                p = jnp.exp(s - m)
                denom = jnp.sum(p, axis=-1, keepdims=True) + jnp.exp(sink - m)
                o = jnp.dot(p.astype(BF16), v_band, preferred_element_type=F32)
                outs.append(o / denom)
        attn_ref[rows, :] = jnp.concatenate(outs, axis=-1).astype(BF16)

        zc = z_ref[rows, :].astype(F32)
        mu = jnp.mean(zc, axis=-1, keepdims=True)
        zc = zc - mu
        var = jnp.mean(zc * zc, axis=-1, keepdims=True)
        zl = (zc * lax.rsqrt(var + EPS) * lng_ref[...] + lnb_ref[...]).astype(BF16)
        mixed = [
            jnp.dot(ws_buf[g], zl[:, g * SGU_GROUP_DIM:(g + 1) * SGU_GROUP_DIM],
                    preferred_element_type=F32)
            for g in range(SGU_GROUPS)
        ]
        mixed = jnp.concatenate(mixed, axis=-1) + bs_ref[...]
        sgu_ref[rows, :] = (u_ref[rows, :].astype(F32) * mixed).astype(BF16)
        return carry

    lax.fori_loop(0, MIX_SUB, block_body, 0)


def _mixer(qkv, u, z, sinks, ws, bs_full, ln_g, ln_b):
    n = qkv.shape[0]
    return pl.pallas_call(
        _mixer_kernel,
        out_shape=(
            jax.ShapeDtypeStruct((n, ATTN_WIDTH), BF16),
            jax.ShapeDtypeStruct((n, SGU_WIDTH), BF16),
        ),
        grid=(n // MIX_BM,),
        in_specs=[
            pl.BlockSpec(memory_space=pltpu.SMEM),
            pl.BlockSpec((MIX_BM, QKV_WIDTH), lambda i: (i, 0)),
            pl.BlockSpec((BLOCK, 2 * KV_WIDTH),
                         lambda i: (jnp.maximum(i * MIX_SUB - 1, 0), ATTN_WIDTH // (2 * KV_WIDTH))),
            pl.BlockSpec((MIX_BM, SGU_WIDTH), lambda i: (i, 0)),
            pl.BlockSpec((MIX_BM, SGU_WIDTH), lambda i: (i, 0)),
            pl.BlockSpec((SGU_GROUPS, BLOCK, BLOCK), lambda i: (0, 0, 0)),
            pl.BlockSpec((BLOCK, SGU_WIDTH), lambda i: (0, 0)),
            pl.BlockSpec((1, SGU_WIDTH), lambda i: (0, 0)),
            pl.BlockSpec((1, SGU_WIDTH), lambda i: (0, 0)),
        ],
        out_specs=(
            pl.BlockSpec((MIX_BM, ATTN_WIDTH), lambda i: (i, 0)),
            pl.BlockSpec((MIX_BM, SGU_WIDTH), lambda i: (i, 0)),
        ),
        scratch_shapes=[
            pltpu.VMEM((MIX_BM + BLOCK, 2 * KV_WIDTH), BF16),
            pltpu.VMEM((SGU_GROUPS, BLOCK, BLOCK), BF16),
        ],
        compiler_params=pltpu.CompilerParams(
            dimension_semantics=("arbitrary",),
            vmem_limit_bytes=VMEM_LIMIT),
        name="mixer",
    )(sinks, qkv, qkv, u, z, ws, bs_full, ln_g, ln_b)


def _merge_kernel(attn_ref, sgu_ref, gt_ref, x_ref, pa_ref, pb_ref, wo_ref, o_ref, m_ref):
    attn = attn_ref[...]
    sgu = sgu_ref[...]
    for c in range(D_MODEL // MERGE_BN):
        cols = slice(c * MERGE_BN, (c + 1) * MERGE_BN)
        a = jnp.dot(attn, pa_ref[:, cols], preferred_element_type=F32)
        b = jnp.dot(sgu, pb_ref[:, cols], preferred_element_type=F32)
        g_a = gt_ref[:, cols].astype(F32)
        g_b = gt_ref[:, D_MODEL + c * MERGE_BN:D_MODEL + (c + 1) * MERGE_BN].astype(F32)
        m_ref[:, cols] = (g_a * a + g_b * b).astype(BF16)
    o_ref[...] = x_ref[...] + jnp.dot(m_ref[...], wo_ref[...], preferred_element_type=F32)


def _merge(attn, sgu, gates, x1, p_a, p_b, w_out):
    n = x1.shape[0]
    const = lambda i: (0, 0)
    return pl.pallas_call(
        _merge_kernel,
        out_shape=jax.ShapeDtypeStruct((n, D_MODEL), F32),
        grid=(n // MERGE_BM,),
        in_specs=[
            pl.BlockSpec((MERGE_BM, ATTN_WIDTH), lambda i: (i, 0)),
            pl.BlockSpec((MERGE_BM, SGU_WIDTH), lambda i: (i, 0)),
            pl.BlockSpec((MERGE_BM, 2 * D_MODEL), lambda i: (i, 0)),
            pl.BlockSpec((MERGE_BM, D_MODEL), lambda i: (i, 0)),
            pl.BlockSpec((ATTN_WIDTH, D_MODEL), const, pipeline_mode=pl.Buffered(1)),
            pl.BlockSpec((SGU_WIDTH, D_MODEL), const, pipeline_mode=pl.Buffered(1)),
            pl.BlockSpec((D_MODEL, D_MODEL), const, pipeline_mode=pl.Buffered(1)),
        ],
        out_specs=pl.BlockSpec((MERGE_BM, D_MODEL), lambda i: (i, 0)),
        scratch_shapes=[pltpu.VMEM((MERGE_BM, D_MODEL), BF16)],
        compiler_params=pltpu.CompilerParams(
            dimension_semantics=("arbitrary",),
            vmem_limit_bytes=VMEM_LIMIT),
        name="merge",
    )(attn, sgu, gates, x1, p_a, p_b, w_out)


def kernel(x, ffn1_norm, ffn1_w_gate, ffn1_w_up, ffn1_w_down, mix_norm, w_in, attn_sinks,
           sgu_norm_g, sgu_norm_b, sgu_w_s, sgu_b_s, w_proj_attn, w_proj_sgu,
           w_branch_gate, b_branch_gate, w_out, ffn2_norm, ffn2_w_gate, ffn2_w_up,
           ffn2_w_down, final_norm):
    batch, seq, d = x.shape
    assert (seq, d) == (SEQ, D_MODEL) and ffn1_norm.shape[0] == 1
    n = batch * seq
    xf = x.reshape(n, d)
    bf = lambda w: w.astype(BF16)
    row = lambda v: v.reshape(1, -1)

    x1, h2 = _ffn(xf, row(ffn1_norm[0]), bf(ffn1_w_gate[0]), bf(ffn1_w_up[0]), bf(ffn1_w_down[0]),
                  row(mix_norm[0]), final_norm=False)

    w_in_bf = bf(w_in[0])
    qkv, u, z = _proj(h2, w_in_bf[:, :QKV_WIDTH], w_in_bf[:, QKV_WIDTH:])
    gates = _gates(h2, bf(w_branch_gate[0]), row(b_branch_gate[0]))

    bs_full = jnp.repeat(sgu_b_s[0].T, SGU_GROUP_DIM, axis=1)
    attn, sgu = _mixer(qkv, u, z, attn_sinks[0], sgu_w_s[0], bs_full,
                       row(sgu_norm_g[0]), row(sgu_norm_b[0]))

    x2 = _merge(attn, sgu, gates, x1, bf(w_proj_attn[0]), bf(w_proj_sgu[0]), bf(w_out[0]))

    (out,) = _ffn(x2, row(ffn2_norm[0]), bf(ffn2_w_gate[0]), bf(ffn2_w_up[0]), bf(ffn2_w_down[0]),
                  row(final_norm), final_norm=True)
    return out.reshape(batch, seq, d)
```

```python
import functools
import math

import jax
import jax.numpy as jnp
from jax import lax
from jax.experimental import pallas as pl
from jax.experimental.pallas import tpu as pltpu

D_MODEL = 2048
SEQ = 2048
HEAD_DIM = 64
N_Q_HEADS = 16
N_KV_HEADS = 4
GQA_GROUP = N_Q_HEADS // N_KV_HEADS
BLOCK = 128
ATTN_WIDTH = N_Q_HEADS * HEAD_DIM
KV_WIDTH = N_KV_HEADS * HEAD_DIM
QKV_WIDTH = ATTN_WIDTH + 2 * KV_WIDTH
SGU_GROUPS = 16
SGU_WIDTH = 1024
SGU_GROUP_DIM = SGU_WIDTH // SGU_GROUPS
D_FF = 5632
EPS = 1e-6
NEG_INF = -1e30

BF16 = jnp.bfloat16
F32 = jnp.float32

FFN_BM = 1024
FFN_BF = 512
PROJ_BM = 1024
PROJ_BN = 512
MIX_BM = 1024
MERGE_BM = 512
MERGE_BN = 512
NORM_ROWS = 128
VMEM_LIMIT = 56 * 1024 * 1024

ALIBI_SLOPES = tuple(2.0 ** (-8.0 * (i + 1) / N_Q_HEADS) for i in range(N_Q_HEADS))


def _rms_scale(y, g):
    ms = jnp.mean(y * y, axis=-1, keepdims=True)
    return y * lax.rsqrt(ms + EPS) * g


N_FF_CHUNKS = D_FF // FFN_BF


def _ffn_kernel(*refs, final_norm):
    if final_norm:
        (x_hbm, g_ref, wg_hbm, wu_hbm, wd_hbm, ng_ref, o_ref,
         x_buf, h_ref, wg_buf, wu_buf, wd_buf, x_sem, w_sem) = refs
        hn_ref = None
    else:
        (x_hbm, g_ref, wg_hbm, wu_hbm, wd_hbm, ng_ref, o_ref, hn_ref,
         x_buf, h_ref, wg_buf, wu_buf, wd_buf, x_sem, w_sem) = refs
    i = pl.program_id(0)
    n_blocks = pl.num_programs(0)

    def x_copy(block):
        rows = pl.ds(pl.multiple_of(block * FFN_BM, FFN_BM), FFN_BM)
        return pltpu.make_async_copy(x_hbm.at[rows, :], x_buf, x_sem.at[0])

    def w_copies(chunk, slot):
        return (
            pltpu.make_async_copy(wg_hbm.at[chunk], wg_buf.at[slot], w_sem.at[0, slot]),
            pltpu.make_async_copy(wu_hbm.at[chunk], wu_buf.at[slot], w_sem.at[1, slot]),
            pltpu.make_async_copy(wd_hbm.at[chunk], wd_buf.at[slot], w_sem.at[2, slot]),
        )

    first_slot = lax.rem(i * N_FF_CHUNKS, 2)

    @pl.when(i == 0)
    def _():
        x_copy(0).start()
        for cp in w_copies(0, 0):
            cp.start()

    x_copy(i).wait()

    def norm_body(r, carry):
        rows = pl.ds(pl.multiple_of(r * NORM_ROWS, NORM_ROWS), NORM_ROWS)
        xs = x_buf[rows, :]
        h_ref[rows, :] = _rms_scale(xs, g_ref[...]).astype(BF16)
        o_ref[rows, :] = xs
        return carry
    lax.fori_loop(0, FFN_BM // NORM_ROWS, norm_body, 0)

    @pl.when(i + 1 < n_blocks)
    def _():
        x_copy(i + 1).start()

    def chunk_body(c, carry):
        slot = lax.rem(first_slot + c, 2)
        for cp in w_copies(c, slot):
            cp.wait()

        @pl.when(c + 1 < N_FF_CHUNKS)
        def _():
            for cp in w_copies(c + 1, 1 - slot):
                cp.start()

        @pl.when(jnp.logical_and(c + 1 == N_FF_CHUNKS, i + 1 < n_blocks))
        def _():
            for cp in w_copies(0, 1 - slot):
                cp.start()

        h = h_ref[...]
        gate = jnp.dot(h, wg_buf[slot], preferred_element_type=F32)
        up = jnp.dot(h, wu_buf[slot], preferred_element_type=F32)
        act = (gate * jax.nn.sigmoid(gate) * up * 0.5).astype(BF16)
        o_ref[...] += jnp.dot(act, wd_buf[slot], preferred_element_type=F32)
        return carry
    lax.fori_loop(0, N_FF_CHUNKS, chunk_body, 0)

    def final_body(r, carry):
        rows = pl.ds(pl.multiple_of(r * NORM_ROWS, NORM_ROWS), NORM_ROWS)
        normed = _rms_scale(o_ref[rows, :], ng_ref[...])
        if final_norm:
            o_ref[rows, :] = normed
        else:
            hn_ref[rows, :] = normed.astype(BF16)
        return carry
    lax.fori_loop(0, FFN_BM // NORM_ROWS, final_body, 0)


def _ffn(x, norm_g, w_gate, w_up, w_down, next_g, *, final_norm):
    n = x.shape[0]
    row_block = lambda i: (i, 0)
    const = lambda i: (0, 0)
    hbm = pl.BlockSpec(memory_space=pl.ANY)
    out_shape = [jax.ShapeDtypeStruct((n, D_MODEL), F32)]
    out_specs = [pl.BlockSpec((FFN_BM, D_MODEL), row_block)]
    if not final_norm:
        out_shape.append(jax.ShapeDtypeStruct((n, D_MODEL), BF16))
        out_specs.append(pl.BlockSpec((FFN_BM, D_MODEL), row_block))
    return pl.pallas_call(
        functools.partial(_ffn_kernel, final_norm=final_norm),
        out_shape=tuple(out_shape),
        grid=(n // FFN_BM,),
        in_specs=[hbm, pl.BlockSpec((1, D_MODEL), const), hbm, hbm, hbm,
                  pl.BlockSpec((1, D_MODEL), const)],
        out_specs=tuple(out_specs),
        scratch_shapes=[
            pltpu.VMEM((FFN_BM, D_MODEL), F32),
            pltpu.VMEM((FFN_BM, D_MODEL), BF16),
            pltpu.VMEM((2, D_MODEL, FFN_BF), BF16),
            pltpu.VMEM((2, D_MODEL, FFN_BF), BF16),
            pltpu.VMEM((2, FFN_BF, D_MODEL), BF16),
            pltpu.SemaphoreType.DMA((1,)),
            pltpu.SemaphoreType.DMA((3, 2)),
        ],
        compiler_params=pltpu.CompilerParams(
            dimension_semantics=("arbitrary",),
            vmem_limit_bytes=VMEM_LIMIT),
        name="ffn_final" if final_norm else "ffn",
    )(x, norm_g, w_gate, w_up, w_down, next_g)


def _gelu_tanh(x):
    c = math.sqrt(2.0 / math.pi)
    return 0.5 * x * (1.0 + jnp.tanh(c * (x + 0.044715 * (x * x * x))))


def _proj_kernel(h_ref, wqkv_ref, wuz_ref, qkv_ref, u_ref, z_ref):
    h = h_ref[...]
    for c in range(QKV_WIDTH // PROJ_BN):
        cols = slice(c * PROJ_BN, (c + 1) * PROJ_BN)
        acc = jnp.dot(h, wqkv_ref[:, cols], preferred_element_type=F32)
        if (c + 1) * PROJ_BN <= ATTN_WIDTH:
            acc = acc * (HEAD_DIM ** -0.5)
        qkv_ref[:, cols] = acc.astype(BF16)
    for c in range(2 * SGU_WIDTH // PROJ_BN):
        acc = jnp.dot(h, wuz_ref[:, c * PROJ_BN:(c + 1) * PROJ_BN], preferred_element_type=F32)
        dst = u_ref if c * PROJ_BN < SGU_WIDTH else z_ref
        off = (c * PROJ_BN) % SGU_WIDTH
        dst[:, off:off + PROJ_BN] = _gelu_tanh(acc).astype(BF16)


def _proj(h2, w_qkv, w_uz):
    n = h2.shape[0]
    row_block = lambda i: (i, 0)
    const = lambda i: (0, 0)
    return pl.pallas_call(
        _proj_kernel,
        out_shape=(
            jax.ShapeDtypeStruct((n, QKV_WIDTH), BF16),
            jax.ShapeDtypeStruct((n, SGU_WIDTH), BF16),
            jax.ShapeDtypeStruct((n, SGU_WIDTH), BF16),
        ),
        grid=(n // PROJ_BM,),
        in_specs=[
            pl.BlockSpec((PROJ_BM, D_MODEL), row_block),
            pl.BlockSpec((D_MODEL, QKV_WIDTH), const, pipeline_mode=pl.Buffered(1)),
            pl.BlockSpec((D_MODEL, 2 * SGU_WIDTH), const, pipeline_mode=pl.Buffered(1)),
        ],
        out_specs=(
            pl.BlockSpec((PROJ_BM, QKV_WIDTH), row_block),
            pl.BlockSpec((PROJ_BM, SGU_WIDTH), row_block),
            pl.BlockSpec((PROJ_BM, SGU_WIDTH), row_block),
        ),
        compiler_params=pltpu.CompilerParams(
            dimension_semantics=("arbitrary",),
            vmem_limit_bytes=VMEM_LIMIT),
        name="proj",
    )(h2, w_qkv, w_uz)


def _gates_kernel(h_ref, w_ref, b_ref, gt_ref):
    h = h_ref[...]
    for c in range(2 * D_MODEL // PROJ_BN):
        cols = slice(c * PROJ_BN, (c + 1) * PROJ_BN)
        acc = jnp.dot(h, w_ref[:, cols], preferred_element_type=F32)
        gt_ref[:, cols] = jax.nn.sigmoid(acc + b_ref[:, cols]).astype(BF16)


def _gates(h2, w_gate, b_gate):
    n = h2.shape[0]
    const = lambda i: (0, 0)
    return pl.pallas_call(
        _gates_kernel,
        out_shape=jax.ShapeDtypeStruct((n, 2 * D_MODEL), BF16),
        grid=(n // PROJ_BM,),
        in_specs=[
            pl.BlockSpec((PROJ_BM, D_MODEL), lambda i: (i, 0)),
            pl.BlockSpec((D_MODEL, 2 * D_MODEL), const, pipeline_mode=pl.Buffered(1)),
            pl.BlockSpec((1, 2 * D_MODEL), const),
        ],
        out_specs=pl.BlockSpec((PROJ_BM, 2 * D_MODEL), lambda i: (i, 0)),
        compiler_params=pltpu.CompilerParams(
            dimension_semantics=("arbitrary",),
            vmem_limit_bytes=VMEM_LIMIT),
        name="gates",
    )(h2, w_gate, b_gate)


MIX_SUB = MIX_BM // BLOCK
K_OFF = ATTN_WIDTH
V_OFF = ATTN_WIDTH + KV_WIDTH


def _mixer_kernel(sinks_ref, qkv_ref, halo_ref, u_ref, z_ref, ws_ref, bs_ref, lng_ref, lnb_ref,
                  attn_ref, sgu_ref, kv_buf, ws_buf):
    i = pl.program_id(0)

    kv_buf[0:BLOCK, :] = halo_ref[...]
    kv_buf[BLOCK:, :] = qkv_ref[:, K_OFF:]

    tq = lax.broadcasted_iota(jnp.int32, (BLOCK, BLOCK), 0)
    ts = lax.broadcasted_iota(jnp.int32, (BLOCK, BLOCK), 1)
    causal = ts <= tq
    for g in range(SGU_GROUPS):
        ws_buf[g] = jnp.where(causal, ws_ref[g], 0.0).astype(BF16)

    qpos = lax.broadcasted_iota(jnp.int32, (BLOCK, 2 * BLOCK), 0) + BLOCK
    kpos = lax.broadcasted_iota(jnp.int32, (BLOCK, 2 * BLOCK), 1)
    dist_i = qpos - kpos
    in_window = (dist_i >= 0) & (dist_i < BLOCK)
    dist = jnp.abs(dist_i).astype(F32)
    step_starts_sequence = (i % (SEQ // MIX_BM)) == 0

    def block_body(r, carry):
        row0 = pl.multiple_of(r * BLOCK, BLOCK)
        rows = pl.ds(row0, BLOCK)
        band = pl.ds(row0, 2 * BLOCK)

        has_prev = jnp.logical_not(jnp.logical_and(step_starts_sequence, r == 0))
        valid = in_window & ((kpos >= BLOCK) | has_prev)

        outs = []
        for hk in range(N_KV_HEADS):
            k_band = kv_buf[band, hk * HEAD_DIM:(hk + 1) * HEAD_DIM]
            v_band = kv_buf[band, KV_WIDTH + hk * HEAD_DIM:KV_WIDTH + (hk + 1) * HEAD_DIM]
            for gq in range(GQA_GROUP):
                h = hk * GQA_GROUP + gq
                q_h = qkv_ref[rows, h * HEAD_DIM:(h + 1) * HEAD_DIM]
                s = lax.dot_general(q_h, k_band, (((1,), (1,)), ((), ())),
                                    preferred_element_type=F32)
                s = jnp.where(valid, s - ALIBI_SLOPES[h] * dist, NEG_INF)
                sink = sinks_ref[h]
                m = jnp.maximum(jnp.max(s, axis=-1, keepdims=True), sink)
                p = jnp.exp(s - m)
                denom = jnp.sum(p, axis=-1, keepdims=True) + jnp.exp(sink - m)
                o = jnp.dot(p.astype(BF16), v_band, preferred_element_type=F32)
                outs.append(o / denom)
        attn_ref[rows, :] = jnp.concatenate(outs, axis=-1).astype(BF16)

        zc = z_ref[rows, :].astype(F32)
        mu = jnp.mean(zc, axis=-1, keepdims=True)
        zc = zc - mu
        var = jnp.mean(zc * zc, axis=-1, keepdims=True)
        zl = (zc * lax.rsqrt(var + EPS) * lng_ref[...] + lnb_ref[...]).astype(BF16)
        mixed = [
            jnp.dot(ws_buf[g], zl[:, g * SGU_GROUP_DIM:(g + 1) * SGU_GROUP_DIM],
                    preferred_element_type=F32)
            for g in range(SGU_GROUPS)
        ]
        mixed = jnp.concatenate(mixed, axis=-1) + bs_ref[...]
        sgu_ref[rows, :] = (u_ref[rows, :].astype(F32) * mixed).astype(BF16)
        return carry

    lax.fori_loop(0, MIX_SUB, block_body, 0)


def _mixer(qkv, u, z, sinks, ws, bs_full, ln_g, ln_b):
    n = qkv.shape[0]
    return pl.pallas_call(
        _mixer_kernel,
        out_shape=(
            jax.ShapeDtypeStruct((n, ATTN_WIDTH), BF16),
            jax.ShapeDtypeStruct((n, SGU_WIDTH), BF16),
        ),
        grid=(n // MIX_BM,),
        in_specs=[
            pl.BlockSpec(memory_space=pltpu.SMEM),
            pl.BlockSpec((MIX_BM, QKV_WIDTH), lambda i: (i, 0)),
            pl.BlockSpec((BLOCK, 2 * KV_WIDTH),
                         lambda i: (jnp.maximum(i * MIX_SUB - 1, 0), ATTN_WIDTH // (2 * KV_WIDTH))),
            pl.BlockSpec((MIX_BM, SGU_WIDTH), lambda i: (i, 0)),
            pl.BlockSpec((MIX_BM, SGU_WIDTH), lambda i: (i, 0)),
            pl.BlockSpec((SGU_GROUPS, BLOCK, BLOCK), lambda i: (0, 0, 0)),
            pl.BlockSpec((BLOCK, SGU_WIDTH), lambda i: (0, 0)),
            pl.BlockSpec((1, SGU_WIDTH), lambda i: (0, 0)),
            pl.BlockSpec((1, SGU_WIDTH), lambda i: (0, 0)),
        ],
        out_specs=(
            pl.BlockSpec((MIX_BM, ATTN_WIDTH), lambda i: (i, 0)),
            pl.BlockSpec((MIX_BM, SGU_WIDTH), lambda i: (i, 0)),
        ),
        scratch_shapes=[
            pltpu.VMEM((MIX_BM + BLOCK, 2 * KV_WIDTH), BF16),
            pltpu.VMEM((SGU_GROUPS, BLOCK, BLOCK), BF16),
        ],
        compiler_params=pltpu.CompilerParams(
            dimension_semantics=("arbitrary",),
            vmem_limit_bytes=VMEM_LIMIT),
        name="mixer",
    )(sinks, qkv, qkv, u, z, ws, bs_full, ln_g, ln_b)


def _merge_kernel(attn_ref, sgu_ref, gt_ref, x_ref, pa_ref, pb_ref, wo_ref, o_ref, m_ref):
    attn = attn_ref[...]
    sgu = sgu_ref[...]
    for c in range(D_MODEL // MERGE_BN):
        cols = slice(c * MERGE_BN, (c + 1) * MERGE_BN)
        a = jnp.dot(attn, pa_ref[:, cols], preferred_element_type=F32)
        b = jnp.dot(sgu, pb_ref[:, cols], preferred_element_type=F32)
        g_a = gt_ref[:, cols].astype(F32)
        g_b = gt_ref[:, D_MODEL + c * MERGE_BN:D_MODEL + (c + 1) * MERGE_BN].astype(F32)
        m_ref[:, cols] = (g_a * a + g_b * b).astype(BF16)
    o_ref[...] = x_ref[...] + jnp.dot(m_ref[...], wo_ref[...], preferred_element_type=F32)


def _merge(attn, sgu, gates, x1, p_a, p_b, w_out):
    n = x1.shape[0]
    const = lambda i: (0, 0)
    return pl.pallas_call(
        _merge_kernel,
        out_shape=jax.ShapeDtypeStruct((n, D_MODEL), F32),
        grid=(n // MERGE_BM,),
        in_specs=[
            pl.BlockSpec((MERGE_BM, ATTN_WIDTH), lambda i: (i, 0)),
            pl.BlockSpec((MERGE_BM, SGU_WIDTH), lambda i: (i, 0)),
            pl.BlockSpec((MERGE_BM, 2 * D_MODEL), lambda i: (i, 0)),
            pl.BlockSpec((MERGE_BM, D_MODEL), lambda i: (i, 0)),
            pl.BlockSpec((ATTN_WIDTH, D_MODEL), const, pipeline_mode=pl.Buffered(1)),
            pl.BlockSpec((SGU_WIDTH, D_MODEL), const, pipeline_mode=pl.Buffered(1)),
            pl.BlockSpec((D_MODEL, D_MODEL), const, pipeline_mode=pl.Buffered(1)),
        ],
        out_specs=pl.BlockSpec((MERGE_BM, D_MODEL), lambda i: (i, 0)),
        scratch_shapes=[pltpu.VMEM((MERGE_BM, D_MODEL), BF16)],
        compiler_params=pltpu.CompilerParams(
            dimension_semantics=("arbitrary",),
            vmem_limit_bytes=VMEM_LIMIT),
        name="merge",
    )(attn, sgu, gates, x1, p_a, p_b, w_out)


def kernel(x, ffn1_norm, ffn1_w_gate, ffn1_w_up, ffn1_w_down, mix_norm, w_in, attn_sinks,
           sgu_norm_g, sgu_norm_b, sgu_w_s, sgu_b_s, w_proj_attn, w_proj_sgu,
           w_branch_gate, b_branch_gate, w_out, ffn2_norm, ffn2_w_gate, ffn2_w_up,
           ffn2_w_down, final_norm):
    batch, seq, d = x.shape
    assert (seq, d) == (SEQ, D_MODEL) and ffn1_norm.shape[0] == 1
    n = batch * seq
    xf = x.reshape(n, d)
    bf = lambda w: w.astype(BF16)
    row = lambda v: v.reshape(1, -1)

    up_chunks = lambda w: bf(w).reshape(D_MODEL, N_FF_CHUNKS, FFN_BF).transpose(1, 0, 2)
    down_chunks = lambda w: bf(w).reshape(N_FF_CHUNKS, FFN_BF, D_MODEL)

    x1, h2 = _ffn(xf, row(ffn1_norm[0]), up_chunks(ffn1_w_gate[0]), up_chunks(ffn1_w_up[0]),
                  down_chunks(ffn1_w_down[0]), row(mix_norm[0]), final_norm=False)

    w_in_bf = bf(w_in[0])
    qkv, u, z = _proj(h2, w_in_bf[:, :QKV_WIDTH], w_in_bf[:, QKV_WIDTH:])
    gates = _gates(h2, bf(w_branch_gate[0]), row(b_branch_gate[0]))

    bs_full = jnp.repeat(sgu_b_s[0].T, SGU_GROUP_DIM, axis=1)
    attn, sgu = _mixer(qkv, u, z, attn_sinks[0], sgu_w_s[0], bs_full,
                       row(sgu_norm_g[0]), row(sgu_norm_b[0]))

    x2 = _merge(attn, sgu, gates, x1, bf(w_proj_attn[0]), bf(w_proj_sgu[0]), bf(w_out[0]))

    (out,) = _ffn(x2, row(ffn2_norm[0]), up_chunks(ffn2_w_gate[0]), up_chunks(ffn2_w_up[0]),
                  down_chunks(ffn2_w_down[0]), row(final_norm), final_norm=True)
    return out.reshape(batch, seq, d)
```

```python
import functools
import math

import jax
import jax.numpy as jnp
from jax import lax
from jax.experimental import pallas as pl
from jax.experimental.pallas import tpu as pltpu

D_MODEL = 2048
SEQ = 2048
HEAD_DIM = 64
N_Q_HEADS = 16
N_KV_HEADS = 4
GQA_GROUP = N_Q_HEADS // N_KV_HEADS
BLOCK = 128
ATTN_WIDTH = N_Q_HEADS * HEAD_DIM
KV_WIDTH = N_KV_HEADS * HEAD_DIM
QKV_WIDTH = ATTN_WIDTH + 2 * KV_WIDTH
SGU_GROUPS = 16
SGU_WIDTH = 1024
SGU_GROUP_DIM = SGU_WIDTH // SGU_GROUPS
D_FF = 5632
EPS = 1e-6
NEG_INF = -1e30

BF16 = jnp.bfloat16
F32 = jnp.float32

FFN_BM = 1024
FFN_BF = 512
PROJ_BM = 1024
PROJ_BN = 512
MIX_BM = 1024
MERGE_BM = 512
MERGE_BN = 512
NORM_ROWS = 128
VMEM_LIMIT = 56 * 1024 * 1024

ALIBI_SLOPES = tuple(2.0 ** (-8.0 * (i + 1) / N_Q_HEADS) for i in range(N_Q_HEADS))


def _rms_scale(y, g):
    ms = jnp.mean(y * y, axis=-1, keepdims=True)
    return y * lax.rsqrt(ms + EPS) * g


N_FF_CHUNKS = D_FF // FFN_BF


def _ffn_kernel(*refs, final_norm):
    if final_norm:
        (x_hbm, g_ref, wg_hbm, wu_hbm, wd_hbm, ng_ref, o_ref,
         x_buf, h_ref, wg_buf, wu_buf, wd_buf, x_sem, w_sem) = refs
        hn_ref = None
    else:
        (x_hbm, g_ref, wg_hbm, wu_hbm, wd_hbm, ng_ref, o_ref, hn_ref,
         x_buf, h_ref, wg_buf, wu_buf, wd_buf, x_sem, w_sem) = refs
    i = pl.program_id(0)
    n_blocks = pl.num_programs(0)

    def x_copy(block):
        rows = pl.ds(pl.multiple_of(block * FFN_BM, FFN_BM), FFN_BM)
        return pltpu.make_async_copy(x_hbm.at[rows, :], x_buf, x_sem.at[0])

    def w_copies(chunk, slot):
        span = pl.ds(pl.multiple_of(chunk * FFN_BF, FFN_BF), FFN_BF)
        return (
            pltpu.make_async_copy(wg_hbm.at[:, span], wg_buf.at[slot], w_sem.at[0, slot]),
            pltpu.make_async_copy(wu_hbm.at[:, span], wu_buf.at[slot], w_sem.at[1, slot]),
            pltpu.make_async_copy(wd_hbm.at[span, :], wd_buf.at[slot], w_sem.at[2, slot]),
        )

    first_slot = lax.rem(i * N_FF_CHUNKS, 2)

    @pl.when(i == 0)
    def _():
        x_copy(0).start()
        for cp in w_copies(0, 0):
            cp.start()

    x_copy(i).wait()

    def norm_body(r, carry):
        rows = pl.ds(pl.multiple_of(r * NORM_ROWS, NORM_ROWS), NORM_ROWS)
        xs = x_buf[rows, :]
        h_ref[rows, :] = _rms_scale(xs, g_ref[...]).astype(BF16)
        o_ref[rows, :] = xs
        return carry
    lax.fori_loop(0, FFN_BM // NORM_ROWS, norm_body, 0, unroll=True)

    @pl.when(i + 1 < n_blocks)
    def _():
        x_copy(i + 1).start()

    def chunk_body(c, carry):
        slot = lax.rem(first_slot + c, 2)
        for cp in w_copies(c, slot):
            cp.wait()

        @pl.when(c + 1 < N_FF_CHUNKS)
        def _():
            for cp in w_copies(c + 1, 1 - slot):
                cp.start()

        @pl.when(jnp.logical_and(c + 1 == N_FF_CHUNKS, i + 1 < n_blocks))
        def _():
            for cp in w_copies(0, 1 - slot):
                cp.start()

        h = h_ref[...]
        gate = jnp.dot(h, wg_buf[slot], preferred_element_type=F32)
        up = jnp.dot(h, wu_buf[slot], preferred_element_type=F32)
        act = (gate * jax.nn.sigmoid(gate) * up * 0.5).astype(BF16)
        o_ref[...] += jnp.dot(act, wd_buf[slot], preferred_element_type=F32)
        return carry
    lax.fori_loop(0, N_FF_CHUNKS, chunk_body, 0)

    def final_body(r, carry):
        rows = pl.ds(pl.multiple_of(r * NORM_ROWS, NORM_ROWS), NORM_ROWS)
        normed = _rms_scale(o_ref[rows, :], ng_ref[...])
        if final_norm:
            o_ref[rows, :] = normed
        else:
            hn_ref[rows, :] = normed.astype(BF16)
        return carry
    lax.fori_loop(0, FFN_BM // NORM_ROWS, final_body, 0, unroll=True)


def _ffn(x, norm_g, w_gate, w_up, w_down, next_g, *, final_norm):
    n = x.shape[0]
    row_block = lambda i: (i, 0)
    const = lambda i: (0, 0)
    hbm = pl.BlockSpec(memory_space=pl.ANY)
    out_shape = [jax.ShapeDtypeStruct((n, D_MODEL), F32)]
    out_specs = [pl.BlockSpec((FFN_BM, D_MODEL), row_block)]
    if not final_norm:
        out_shape.append(jax.ShapeDtypeStruct((n, D_MODEL), BF16))
        out_specs.append(pl.BlockSpec((FFN_BM, D_MODEL), row_block))
    return pl.pallas_call(
        functools.partial(_ffn_kernel, final_norm=final_norm),
        out_shape=tuple(out_shape),
        grid=(n // FFN_BM,),
        in_specs=[hbm, pl.BlockSpec((1, D_MODEL), const), hbm, hbm, hbm,
                  pl.BlockSpec((1, D_MODEL), const)],
        out_specs=tuple(out_specs),
        scratch_shapes=[
            pltpu.VMEM((FFN_BM, D_MODEL), F32),
            pltpu.VMEM((FFN_BM, D_MODEL), BF16),
            pltpu.VMEM((2, D_MODEL, FFN_BF), BF16),
            pltpu.VMEM((2, D_MODEL, FFN_BF), BF16),
            pltpu.VMEM((2, FFN_BF, D_MODEL), BF16),
            pltpu.SemaphoreType.DMA((1,)),
            pltpu.SemaphoreType.DMA((3, 2)),
        ],
        compiler_params=pltpu.CompilerParams(
            dimension_semantics=("arbitrary",),
            vmem_limit_bytes=VMEM_LIMIT),
        name="ffn_final" if final_norm else "ffn",
    )(x, norm_g, w_gate, w_up, w_down, next_g)


def _gelu_tanh(x):
    c = math.sqrt(2.0 / math.pi)
    return 0.5 * x * (1.0 + jnp.tanh(c * (x + 0.044715 * (x * x * x))))


def _proj_kernel(h_ref, wqkv_ref, wuz_ref, qkv_ref, u_ref, z_ref):
    h = h_ref[...]
    for c in range(QKV_WIDTH // PROJ_BN):
        cols = slice(c * PROJ_BN, (c + 1) * PROJ_BN)
        acc = jnp.dot(h, wqkv_ref[:, cols], preferred_element_type=F32)
        if (c + 1) * PROJ_BN <= ATTN_WIDTH:
            acc = acc * (HEAD_DIM ** -0.5)
        qkv_ref[:, cols] = acc.astype(BF16)
    for c in range(2 * SGU_WIDTH // PROJ_BN):
        acc = jnp.dot(h, wuz_ref[:, c * PROJ_BN:(c + 1) * PROJ_BN], preferred_element_type=F32)
        dst = u_ref if c * PROJ_BN < SGU_WIDTH else z_ref
        off = (c * PROJ_BN) % SGU_WIDTH
        dst[:, off:off + PROJ_BN] = _gelu_tanh(acc).astype(BF16)


def _proj(h2, w_qkv, w_uz):
    n = h2.shape[0]
    row_block = lambda i: (i, 0)
    const = lambda i: (0, 0)
    return pl.pallas_call(
        _proj_kernel,
        out_shape=(
            jax.ShapeDtypeStruct((n, QKV_WIDTH), BF16),
            jax.ShapeDtypeStruct((n, SGU_WIDTH), BF16),
            jax.ShapeDtypeStruct((n, SGU_WIDTH), BF16),
        ),
        grid=(n // PROJ_BM,),
        in_specs=[
            pl.BlockSpec((PROJ_BM, D_MODEL), row_block),
            pl.BlockSpec((D_MODEL, QKV_WIDTH), const, pipeline_mode=pl.Buffered(1)),
            pl.BlockSpec((D_MODEL, 2 * SGU_WIDTH), const, pipeline_mode=pl.Buffered(1)),
        ],
        out_specs=(
            pl.BlockSpec((PROJ_BM, QKV_WIDTH), row_block),
            pl.BlockSpec((PROJ_BM, SGU_WIDTH), row_block),
            pl.BlockSpec((PROJ_BM, SGU_WIDTH), row_block),
        ),
        compiler_params=pltpu.CompilerParams(
            dimension_semantics=("arbitrary",),
            vmem_limit_bytes=VMEM_LIMIT),
        name="proj",
    )(h2, w_qkv, w_uz)


def _gates_kernel(h_ref, w_ref, b_ref, gt_ref):
    h = h_ref[...]
    for c in range(2 * D_MODEL // PROJ_BN):
        cols = slice(c * PROJ_BN, (c + 1) * PROJ_BN)
        acc = jnp.dot(h, w_ref[:, cols], preferred_element_type=F32)
        gt_ref[:, cols] = jax.nn.sigmoid(acc + b_ref[:, cols]).astype(BF16)


def _gates(h2, w_gate, b_gate):
    n = h2.shape[0]
    const = lambda i: (0, 0)
    return pl.pallas_call(
        _gates_kernel,
        out_shape=jax.ShapeDtypeStruct((n, 2 * D_MODEL), BF16),
        grid=(n // PROJ_BM,),
        in_specs=[
            pl.BlockSpec((PROJ_BM, D_MODEL), lambda i: (i, 0)),
            pl.BlockSpec((D_MODEL, 2 * D_MODEL), const, pipeline_mode=pl.Buffered(1)),
            pl.BlockSpec((1, 2 * D_MODEL), const),
        ],
        out_specs=pl.BlockSpec((PROJ_BM, 2 * D_MODEL), lambda i: (i, 0)),
        compiler_params=pltpu.CompilerParams(
            dimension_semantics=("arbitrary",),
            vmem_limit_bytes=VMEM_LIMIT),
        name="gates",
    )(h2, w_gate, b_gate)


MIX_SUB = MIX_BM // BLOCK
K_OFF = ATTN_WIDTH
V_OFF = ATTN_WIDTH + KV_WIDTH


def _mixer_kernel(sinks_ref, qkv_ref, halo_ref, u_ref, z_ref, ws_ref, bs_ref, lng_ref, lnb_ref,
                  attn_ref, sgu_ref, kv_buf, ws_buf, bias_buf):
    i = pl.program_id(0)

    kv_buf[0:BLOCK, :] = halo_ref[...]
    kv_buf[BLOCK:, :] = qkv_ref[:, K_OFF:]

    @pl.when(i == 0)
    def _():
        tq = lax.broadcasted_iota(jnp.int32, (BLOCK, BLOCK), 0)
        ts = lax.broadcasted_iota(jnp.int32, (BLOCK, BLOCK), 1)
        causal = ts <= tq
        for g in range(SGU_GROUPS):
            ws_buf[g] = jnp.where(causal, ws_ref[g], 0.0).astype(BF16)

        qpos = lax.broadcasted_iota(jnp.int32, (BLOCK, 2 * BLOCK), 0) + BLOCK
        kpos = lax.broadcasted_iota(jnp.int32, (BLOCK, 2 * BLOCK), 1)
        dist_i = qpos - kpos
        in_window = (dist_i >= 0) & (dist_i < BLOCK)
        dist = jnp.abs(dist_i).astype(F32)
        for h in range(N_Q_HEADS):
            alibi = -ALIBI_SLOPES[h] * dist
            bias_buf[0, h] = jnp.where(in_window, alibi, NEG_INF)
            bias_buf[1, h] = jnp.where(in_window & (kpos >= BLOCK), alibi, NEG_INF)

    step_starts_sequence = (i % (SEQ // MIX_BM)) == 0

    def block_body(r, carry):
        row0 = pl.multiple_of(r * BLOCK, BLOCK)
        rows = pl.ds(row0, BLOCK)
        band = pl.ds(row0, 2 * BLOCK)

        no_prev = jnp.logical_and(step_starts_sequence, r == 0).astype(jnp.int32)

        outs = []
        for hk in range(N_KV_HEADS):
            k_band = kv_buf[band, hk * HEAD_DIM:(hk + 1) * HEAD_DIM]
            v_band = kv_buf[band, KV_WIDTH + hk * HEAD_DIM:KV_WIDTH + (hk + 1) * HEAD_DIM]
            for gq in range(GQA_GROUP):
                h = hk * GQA_GROUP + gq
                q_h = qkv_ref[rows, h * HEAD_DIM:(h + 1) * HEAD_DIM]
                s = lax.dot_general(q_h, k_band, (((1,), (1,)), ((), ())),
                                    preferred_element_type=F32)
                s = s + bias_buf[no_prev, h]
                sink = sinks_ref[h]
                m = jnp.maximum(jnp.max(s, axis=-1, keepdims=True), sink)
                p = jnp.exp(s - m)
                denom = jnp.sum(p, axis=-1, keepdims=True) + jnp.exp(sink - m)
                o = jnp.dot(p.astype(BF16), v_band, preferred_element_type=F32)
                outs.append(o / denom)
        attn_ref[rows, :] = jnp.concatenate(outs, axis=-1).astype(BF16)

        zc = z_ref[rows, :].astype(F32)
        mu = jnp.mean(zc, axis=-1, keepdims=True)
        zc = zc - mu
        var = jnp.mean(zc * zc, axis=-1, keepdims=True)
        zl = (zc * lax.rsqrt(var + EPS) * lng_ref[...] + lnb_ref[...]).astype(BF16)
        mixed = [
            jnp.dot(ws_buf[g], zl[:, g * SGU_GROUP_DIM:(g + 1) * SGU_GROUP_DIM],
                    preferred_element_type=F32)
            for g in range(SGU_GROUPS)
        ]
        mixed = jnp.concatenate(mixed, axis=-1) + bs_ref[...]
        sgu_ref[rows, :] = (u_ref[rows, :].astype(F32) * mixed).astype(BF16)
        return carry

    lax.fori_loop(0, MIX_SUB, block_body, 0)


def _mixer(qkv, u, z, sinks, ws, bs_full, ln_g, ln_b):
    n = qkv.shape[0]
    return pl.pallas_call(
        _mixer_kernel,
        out_shape=(
            jax.ShapeDtypeStruct((n, ATTN_WIDTH), BF16),
            jax.ShapeDtypeStruct((n, SGU_WIDTH), BF16),
        ),
        grid=(n // MIX_BM,),
        in_specs=[
            pl.BlockSpec(memory_space=pltpu.SMEM),
            pl.BlockSpec((MIX_BM, QKV_WIDTH), lambda i: (i, 0)),
            pl.BlockSpec((BLOCK, 2 * KV_WIDTH),
                         lambda i: (jnp.maximum(i * MIX_SUB - 1, 0), ATTN_WIDTH // (2 * KV_WIDTH))),
            pl.BlockSpec((MIX_BM, SGU_WIDTH), lambda i: (i, 0)),
            pl.BlockSpec((MIX_BM, SGU_WIDTH), lambda i: (i, 0)),
            pl.BlockSpec((SGU_GROUPS, BLOCK, BLOCK), lambda i: (0, 0, 0)),
            pl.BlockSpec((BLOCK, SGU_WIDTH), lambda i: (0, 0)),
            pl.BlockSpec((1, SGU_WIDTH), lambda i: (0, 0)),
            pl.BlockSpec((1, SGU_WIDTH), lambda i: (0, 0)),
        ],
        out_specs=(
            pl.BlockSpec((MIX_BM, ATTN_WIDTH), lambda i: (i, 0)),
            pl.BlockSpec((MIX_BM, SGU_WIDTH), lambda i: (i, 0)),
        ),
        scratch_shapes=[
            pltpu.VMEM((MIX_BM + BLOCK, 2 * KV_WIDTH), BF16),
            pltpu.VMEM((SGU_GROUPS, BLOCK, BLOCK), BF16),
            pltpu.VMEM((2, N_Q_HEADS, BLOCK, 2 * BLOCK), F32),
        ],
        compiler_params=pltpu.CompilerParams(
            dimension_semantics=("arbitrary",),
            vmem_limit_bytes=VMEM_LIMIT),
        name="mixer",
    )(sinks, qkv, qkv, u, z, ws, bs_full, ln_g, ln_b)


def _merge_kernel(attn_ref, sgu_ref, gt_ref, x_ref, pa_ref, pb_ref, wo_ref, o_ref, m_ref):
    attn = attn_ref[...]
    sgu = sgu_ref[...]
    for c in range(D_MODEL // MERGE_BN):
        cols = slice(c * MERGE_BN, (c + 1) * MERGE_BN)
        a = jnp.dot(attn, pa_ref[:, cols], preferred_element_type=F32)
        b = jnp.dot(sgu, pb_ref[:, cols], preferred_element_type=F32)
        g_a = gt_ref[:, cols].astype(F32)
        g_b = gt_ref[:, D_MODEL + c * MERGE_BN:D_MODEL + (c + 1) * MERGE_BN].astype(F32)
        m_ref[:, cols] = (g_a * a + g_b * b).astype(BF16)
    o_ref[...] = x_ref[...] + jnp.dot(m_ref[...], wo_ref[...], preferred_element_type=F32)


def _merge(attn, sgu, gates, x1, p_a, p_b, w_out):
    n = x1.shape[0]
    const = lambda i: (0, 0)
    return pl.pallas_call(
        _merge_kernel,
        out_shape=jax.ShapeDtypeStruct((n, D_MODEL), F32),
        grid=(n // MERGE_BM,),
        in_specs=[
            pl.BlockSpec((MERGE_BM, ATTN_WIDTH), lambda i: (i, 0)),
            pl.BlockSpec((MERGE_BM, SGU_WIDTH), lambda i: (i, 0)),
            pl.BlockSpec((MERGE_BM, 2 * D_MODEL), lambda i: (i, 0)),
            pl.BlockSpec((MERGE_BM, D_MODEL), lambda i: (i, 0)),
            pl.BlockSpec((ATTN_WIDTH, D_MODEL), const, pipeline_mode=pl.Buffered(1)),
            pl.BlockSpec((SGU_WIDTH, D_MODEL), const, pipeline_mode=pl.Buffered(1)),
            pl.BlockSpec((D_MODEL, D_MODEL), const, pipeline_mode=pl.Buffered(1)),
        ],
        out_specs=pl.BlockSpec((MERGE_BM, D_MODEL), lambda i: (i, 0)),
        scratch_shapes=[pltpu.VMEM((MERGE_BM, D_MODEL), BF16)],
        compiler_params=pltpu.CompilerParams(
            dimension_semantics=("arbitrary",),
            vmem_limit_bytes=VMEM_LIMIT),
        name="merge",
    )(attn, sgu, gates, x1, p_a, p_b, w_out)


def kernel(x, ffn1_norm, ffn1_w_gate, ffn1_w_up, ffn1_w_down, mix_norm, w_in, attn_sinks,
           sgu_norm_g, sgu_norm_b, sgu_w_s, sgu_b_s, w_proj_attn, w_proj_sgu,
           w_branch_gate, b_branch_gate, w_out, ffn2_norm, ffn2_w_gate, ffn2_w_up,
           ffn2_w_down, final_norm):
    batch, seq, d = x.shape
    assert (seq, d) == (SEQ, D_MODEL) and ffn1_norm.shape[0] == 1
    n = batch * seq
    xf = x.reshape(n, d)
    bf = lambda w: w.astype(BF16)
    row = lambda v: v.reshape(1, -1)

    x1, h2 = _ffn(xf, row(ffn1_norm[0]), bf(ffn1_w_gate[0]), bf(ffn1_w_up[0]), bf(ffn1_w_down[0]),
                  row(mix_norm[0]), final_norm=False)

    w_in_bf = bf(w_in[0])
    qkv, u, z = _proj(h2, w_in_bf[:, :QKV_WIDTH], w_in_bf[:, QKV_WIDTH:])
    gates = _gates(h2, bf(w_branch_gate[0]), row(b_branch_gate[0]))

    bs_full = jnp.repeat(sgu_b_s[0].T, SGU_GROUP_DIM, axis=1)
    attn, sgu = _mixer(qkv, u, z, attn_sinks[0], sgu_w_s[0], bs_full,
                       row(sgu_norm_g[0]), row(sgu_norm_b[0]))

    x2 = _merge(attn, sgu, gates, x1, bf(w_proj_attn[0]), bf(w_proj_sgu[0]), bf(w_out[0]))

    (out,) = _ffn(x2, row(ffn2_norm[0]), bf(ffn2_w_gate[0]), bf(ffn2_w_up[0]), bf(ffn2_w_down[0]),
                  row(final_norm), final_norm=True)
    return out.reshape(batch, seq, d)
```

```python
import functools
import math

import jax
import jax.numpy as jnp
from jax import lax
from jax.experimental import pallas as pl
from jax.experimental.pallas import tpu as pltpu

D_MODEL = 2048
SEQ = 2048
HEAD_DIM = 64
N_Q_HEADS = 16
N_KV_HEADS = 4
GQA_GROUP = N_Q_HEADS // N_KV_HEADS
BLOCK = 128
ATTN_WIDTH = N_Q_HEADS * HEAD_DIM
KV_WIDTH = N_KV_HEADS * HEAD_DIM
QKV_WIDTH = ATTN_WIDTH + 2 * KV_WIDTH
SGU_GROUPS = 16
SGU_WIDTH = 1024
SGU_GROUP_DIM = SGU_WIDTH // SGU_GROUPS
D_FF = 5632
EPS = 1e-6
NEG_INF = -1e30

BF16 = jnp.bfloat16
F32 = jnp.float32

FFN_BM = 1024
FFN_BF = 512
PROJ_BM = 1024
PROJ_BN = 512
GM_BM = 256
MERGE_BM = 512
MERGE_BN = 512
NORM_ROWS = 128
VMEM_LIMIT = 56 * 1024 * 1024

ALIBI_SLOPES = tuple(2.0 ** (-8.0 * (i + 1) / N_Q_HEADS) for i in range(N_Q_HEADS))


def _rms_scale(y, g):
    ms = jnp.mean(y * y, axis=-1, keepdims=True)
    return y * lax.rsqrt(ms + EPS) * g


N_FF_CHUNKS = D_FF // FFN_BF


def _ffn_kernel(*refs, final_norm):
    if final_norm:
        (x_hbm, g_ref, wg_hbm, wu_hbm, wd_hbm, ng_ref, o_ref,
         x_buf, h_ref, wg_buf, wu_buf, wd_buf, x_sem, w_sem) = refs
        hn_ref = None
    else:
        (x_hbm, g_ref, wg_hbm, wu_hbm, wd_hbm, ng_ref, o_ref, hn_ref,
         x_buf, h_ref, wg_buf, wu_buf, wd_buf, x_sem, w_sem) = refs
    i = pl.program_id(0)
    n_blocks = pl.num_programs(0)

    def x_copy(block):
        rows = pl.ds(pl.multiple_of(block * FFN_BM, FFN_BM), FFN_BM)
        return pltpu.make_async_copy(x_hbm.at[rows, :], x_buf, x_sem.at[0])

    def w_copies(chunk, slot):
        span = pl.ds(pl.multiple_of(chunk * FFN_BF, FFN_BF), FFN_BF)
        return (
            pltpu.make_async_copy(wg_hbm.at[:, span], wg_buf.at[slot], w_sem.at[0, slot]),
            pltpu.make_async_copy(wu_hbm.at[:, span], wu_buf.at[slot], w_sem.at[1, slot]),
            pltpu.make_async_copy(wd_hbm.at[span, :], wd_buf.at[slot], w_sem.at[2, slot]),
        )

    first_slot = lax.rem(i * N_FF_CHUNKS, 2)

    @pl.when(i == 0)
    def _():
        x_copy(0).start()
        for cp in w_copies(0, 0):
            cp.start()

    x_copy(i).wait()

    def norm_body(r, carry):
        rows = pl.ds(pl.multiple_of(r * NORM_ROWS, NORM_ROWS), NORM_ROWS)
        xs = x_buf[rows, :]
        h_ref[rows, :] = _rms_scale(xs, g_ref[...]).astype(BF16)
        o_ref[rows, :] = xs
        return carry
    lax.fori_loop(0, FFN_BM // NORM_ROWS, norm_body, 0, unroll=True)

    @pl.when(i + 1 < n_blocks)
    def _():
        x_copy(i + 1).start()

    def chunk_body(c, carry):
        slot = lax.rem(first_slot + c, 2)
        for cp in w_copies(c, slot):
            cp.wait()

        @pl.when(c + 1 < N_FF_CHUNKS)
        def _():
            for cp in w_copies(c + 1, 1 - slot):
                cp.start()

        @pl.when(jnp.logical_and(c + 1 == N_FF_CHUNKS, i + 1 < n_blocks))
        def _():
            for cp in w_copies(0, 1 - slot):
                cp.start()

        h = h_ref[...]
        gate = jnp.dot(h, wg_buf[slot], preferred_element_type=F32)
        up = jnp.dot(h, wu_buf[slot], preferred_element_type=F32)
        act = (gate * jax.nn.sigmoid(gate) * up * 0.5).astype(BF16)
        o_ref[...] += jnp.dot(act, wd_buf[slot], preferred_element_type=F32)
        return carry
    lax.fori_loop(0, N_FF_CHUNKS, chunk_body, 0)

    def final_body(r, carry):
        rows = pl.ds(pl.multiple_of(r * NORM_ROWS, NORM_ROWS), NORM_ROWS)
        normed = _rms_scale(o_ref[rows, :], ng_ref[...])
        if final_norm:
            o_ref[rows, :] = normed
        else:
            hn_ref[rows, :] = normed.astype(BF16)
        return carry
    lax.fori_loop(0, FFN_BM // NORM_ROWS, final_body, 0, unroll=True)


def _ffn(x, norm_g, w_gate, w_up, w_down, next_g, *, final_norm):
    n = x.shape[0]
    row_block = lambda i: (i, 0)
    const = lambda i: (0, 0)
    hbm = pl.BlockSpec(memory_space=pl.ANY)
    out_shape = [jax.ShapeDtypeStruct((n, D_MODEL), F32)]
    out_specs = [pl.BlockSpec((FFN_BM, D_MODEL), row_block)]
    if not final_norm:
        out_shape.append(jax.ShapeDtypeStruct((n, D_MODEL), BF16))
        out_specs.append(pl.BlockSpec((FFN_BM, D_MODEL), row_block))
    return pl.pallas_call(
        functools.partial(_ffn_kernel, final_norm=final_norm),
        out_shape=tuple(out_shape),
        grid=(n // FFN_BM,),
        in_specs=[hbm, pl.BlockSpec((1, D_MODEL), const), hbm, hbm, hbm,
                  pl.BlockSpec((1, D_MODEL), const)],
        out_specs=tuple(out_specs),
        scratch_shapes=[
            pltpu.VMEM((FFN_BM, D_MODEL), F32),
            pltpu.VMEM((FFN_BM, D_MODEL), BF16),
            pltpu.VMEM((2, D_MODEL, FFN_BF), BF16),
            pltpu.VMEM((2, D_MODEL, FFN_BF), BF16),
            pltpu.VMEM((2, FFN_BF, D_MODEL), BF16),
            pltpu.SemaphoreType.DMA((1,)),
            pltpu.SemaphoreType.DMA((3, 2)),
        ],
        compiler_params=pltpu.CompilerParams(
            dimension_semantics=("arbitrary",),
            vmem_limit_bytes=VMEM_LIMIT),
        name="ffn_final" if final_norm else "ffn",
    )(x, norm_g, w_gate, w_up, w_down, next_g)


def _gelu_tanh(x):
    c = math.sqrt(2.0 / math.pi)
    return 0.5 * x * (1.0 + jnp.tanh(c * (x + 0.044715 * (x * x * x))))


def _proj_kernel(h_ref, wqkv_ref, wuz_ref, qkv_ref, u_ref, z_ref):
    h = h_ref[...]
    for c in range(QKV_WIDTH // PROJ_BN):
        cols = slice(c * PROJ_BN, (c + 1) * PROJ_BN)
        acc = jnp.dot(h, wqkv_ref[:, cols], preferred_element_type=F32)
        if (c + 1) * PROJ_BN <= ATTN_WIDTH:
            acc = acc * (HEAD_DIM ** -0.5)
        qkv_ref[:, cols] = acc.astype(BF16)
    for c in range(2 * SGU_WIDTH // PROJ_BN):
        acc = jnp.dot(h, wuz_ref[:, c * PROJ_BN:(c + 1) * PROJ_BN], preferred_element_type=F32)
        dst = u_ref if c * PROJ_BN < SGU_WIDTH else z_ref
        off = (c * PROJ_BN) % SGU_WIDTH
        dst[:, off:off + PROJ_BN] = _gelu_tanh(acc).astype(BF16)


def _proj(h2, w_qkv, w_uz):
    n = h2.shape[0]
    row_block = lambda i: (i, 0)
    const = lambda i: (0, 0)
    return pl.pallas_call(
        _proj_kernel,
        out_shape=(
            jax.ShapeDtypeStruct((n, QKV_WIDTH), BF16),
            jax.ShapeDtypeStruct((n, SGU_WIDTH), BF16),
            jax.ShapeDtypeStruct((n, SGU_WIDTH), BF16),
        ),
        grid=(n // PROJ_BM,),
        in_specs=[
            pl.BlockSpec((PROJ_BM, D_MODEL), row_block),
            pl.BlockSpec((D_MODEL, QKV_WIDTH), const, pipeline_mode=pl.Buffered(1)),
            pl.BlockSpec((D_MODEL, 2 * SGU_WIDTH), const, pipeline_mode=pl.Buffered(1)),
        ],
        out_specs=(
            pl.BlockSpec((PROJ_BM, QKV_WIDTH), row_block),
            pl.BlockSpec((PROJ_BM, SGU_WIDTH), row_block),
            pl.BlockSpec((PROJ_BM, SGU_WIDTH), row_block),
        ),
        compiler_params=pltpu.CompilerParams(
            dimension_semantics=("arbitrary",),
            vmem_limit_bytes=VMEM_LIMIT),
        name="proj",
    )(h2, w_qkv, w_uz)


GM_SUB = GM_BM // BLOCK
GM_GATE_CHUNKS = 2 * D_MODEL // PROJ_BN
K_OFF = ATTN_WIDTH


def _mix_block(r, no_prev, sinks_ref, qkv_ref, u_ref, z_ref, bs_ref, lng_ref, lnb_ref,
               attn_ref, sgu_ref, kv_buf, ws_buf, bias_buf):
    rows = slice(r * BLOCK, (r + 1) * BLOCK)
    band = slice(r * BLOCK, (r + 2) * BLOCK)

    outs = []
    for hk in range(N_KV_HEADS):
        k_band = kv_buf[band, hk * HEAD_DIM:(hk + 1) * HEAD_DIM]
        v_band = kv_buf[band, KV_WIDTH + hk * HEAD_DIM:KV_WIDTH + (hk + 1) * HEAD_DIM]
        for gq in range(GQA_GROUP):
            h = hk * GQA_GROUP + gq
            q_h = qkv_ref[rows, h * HEAD_DIM:(h + 1) * HEAD_DIM]
            s = lax.dot_general(q_h, k_band, (((1,), (1,)), ((), ())),
                                preferred_element_type=F32)
            s = s + bias_buf[no_prev, h]
            sink = sinks_ref[h]
            m = jnp.maximum(jnp.max(s, axis=-1, keepdims=True), sink)
            p = jnp.exp(s - m)
            denom = jnp.sum(p, axis=-1, keepdims=True) + jnp.exp(sink - m)
            o = jnp.dot(p.astype(BF16), v_band, preferred_element_type=F32)
            outs.append(o / denom)
    attn_ref[rows, :] = jnp.concatenate(outs, axis=-1).astype(BF16)

    zc = z_ref[rows, :].astype(F32)
    mu = jnp.mean(zc, axis=-1, keepdims=True)
    zc = zc - mu
    var = jnp.mean(zc * zc, axis=-1, keepdims=True)
    zl = (zc * lax.rsqrt(var + EPS) * lng_ref[...] + lnb_ref[...]).astype(BF16)
    mixed = [
        jnp.dot(ws_buf[g], zl[:, g * SGU_GROUP_DIM:(g + 1) * SGU_GROUP_DIM],
                preferred_element_type=F32)
        for g in range(SGU_GROUPS)
    ]
    mixed = jnp.concatenate(mixed, axis=-1) + bs_ref[...]
    sgu_ref[rows, :] = (u_ref[rows, :].astype(F32) * mixed).astype(BF16)


def _gates_mixer_kernel(sinks_ref, h_ref, wg_ref, bg_ref, qkv_ref, halo_ref, u_ref, z_ref, ws_ref,
                        bs_ref, lng_ref, lnb_ref, gt_ref, attn_ref, sgu_ref,
                        kv_buf, ws_buf, bias_buf):
    i = pl.program_id(0)

    @pl.when(i == 0)
    def _():
        tq = lax.broadcasted_iota(jnp.int32, (BLOCK, BLOCK), 0)
        ts = lax.broadcasted_iota(jnp.int32, (BLOCK, BLOCK), 1)
        causal = ts <= tq
        for g in range(SGU_GROUPS):
            ws_buf[g] = jnp.where(causal, ws_ref[g], 0.0).astype(BF16)

        qpos = lax.broadcasted_iota(jnp.int32, (BLOCK, 2 * BLOCK), 0) + BLOCK
        kpos = lax.broadcasted_iota(jnp.int32, (BLOCK, 2 * BLOCK), 1)
        dist_i = qpos - kpos
        in_window = (dist_i >= 0) & (dist_i < BLOCK)
        dist = jnp.abs(dist_i).astype(F32)
        for h in range(N_Q_HEADS):
            alibi = -ALIBI_SLOPES[h] * dist
            bias_buf[0, h] = jnp.where(in_window, alibi, NEG_INF)
            bias_buf[1, h] = jnp.where(in_window & (kpos >= BLOCK), alibi, NEG_INF)

    kv_buf[0:BLOCK, :] = halo_ref[...]
    kv_buf[BLOCK:, :] = qkv_ref[:, K_OFF:]

    step_starts_sequence = ((i % (SEQ // GM_BM)) == 0).astype(jnp.int32)

    h = h_ref[...]
    chunks_per_block = GM_GATE_CHUNKS // GM_SUB
    for c in range(GM_GATE_CHUNKS):
        cols = slice(c * PROJ_BN, (c + 1) * PROJ_BN)
        acc = jnp.dot(h, wg_ref[:, cols], preferred_element_type=F32)
        gt_ref[:, cols] = jax.nn.sigmoid(acc + bg_ref[:, cols]).astype(BF16)
        if c % chunks_per_block == 0:
            r = c // chunks_per_block
            _mix_block(r, step_starts_sequence if r == 0 else 0,
                       sinks_ref, qkv_ref, u_ref, z_ref, bs_ref, lng_ref, lnb_ref,
                       attn_ref, sgu_ref, kv_buf, ws_buf, bias_buf)


def _gates_mixer(h2, w_gate, b_gate, qkv, u, z, sinks, ws, bs_full, ln_g, ln_b):
    n = h2.shape[0]
    row_block = lambda i: (i, 0)
    const = lambda i: (0, 0)
    return pl.pallas_call(
        _gates_mixer_kernel,
        out_shape=(
            jax.ShapeDtypeStruct((n, 2 * D_MODEL), BF16),
            jax.ShapeDtypeStruct((n, ATTN_WIDTH), BF16),
            jax.ShapeDtypeStruct((n, SGU_WIDTH), BF16),
        ),
        grid=(n // GM_BM,),
        in_specs=[
            pl.BlockSpec(memory_space=pltpu.SMEM),
            pl.BlockSpec((GM_BM, D_MODEL), row_block),
            pl.BlockSpec((D_MODEL, 2 * D_MODEL), const, pipeline_mode=pl.Buffered(1)),
            pl.BlockSpec((1, 2 * D_MODEL), const),
            pl.BlockSpec((GM_BM, QKV_WIDTH), row_block),
            pl.BlockSpec((BLOCK, 2 * KV_WIDTH),
                         lambda i: (jnp.maximum(i * GM_SUB - 1, 0), ATTN_WIDTH // (2 * KV_WIDTH))),
            pl.BlockSpec((GM_BM, SGU_WIDTH), row_block),
            pl.BlockSpec((GM_BM, SGU_WIDTH), row_block),
            pl.BlockSpec((SGU_GROUPS, BLOCK, BLOCK), lambda i: (0, 0, 0)),
            pl.BlockSpec((BLOCK, SGU_WIDTH), const),
            pl.BlockSpec((1, SGU_WIDTH), const),
            pl.BlockSpec((1, SGU_WIDTH), const),
        ],
        out_specs=(
            pl.BlockSpec((GM_BM, 2 * D_MODEL), row_block),
            pl.BlockSpec((GM_BM, ATTN_WIDTH), row_block),
            pl.BlockSpec((GM_BM, SGU_WIDTH), row_block),
        ),
        scratch_shapes=[
            pltpu.VMEM((GM_BM + BLOCK, 2 * KV_WIDTH), BF16),
            pltpu.VMEM((SGU_GROUPS, BLOCK, BLOCK), BF16),
            pltpu.VMEM((2, N_Q_HEADS, BLOCK, 2 * BLOCK), F32),
        ],
        compiler_params=pltpu.CompilerParams(
            dimension_semantics=("arbitrary",),
            vmem_limit_bytes=VMEM_LIMIT),
        name="gates_mixer",
    )(sinks, h2, w_gate, b_gate, qkv, qkv, u, z, ws, bs_full, ln_g, ln_b)


def _merge_kernel(attn_ref, sgu_ref, gt_ref, x_ref, pa_ref, pb_ref, wo_ref, o_ref, m_ref):
    attn = attn_ref[...]
    sgu = sgu_ref[...]
    for c in range(D_MODEL // MERGE_BN):
        cols = slice(c * MERGE_BN, (c + 1) * MERGE_BN)
        a = jnp.dot(attn, pa_ref[:, cols], preferred_element_type=F32)
        b = jnp.dot(sgu, pb_ref[:, cols], preferred_element_type=F32)
        g_a = gt_ref[:, cols].astype(F32)
        g_b = gt_ref[:, D_MODEL + c * MERGE_BN:D_MODEL + (c + 1) * MERGE_BN].astype(F32)
        m_ref[:, cols] = (g_a * a + g_b * b).astype(BF16)
    o_ref[...] = x_ref[...] + jnp.dot(m_ref[...], wo_ref[...], preferred_element_type=F32)


def _merge(attn, sgu, gates, x1, p_a, p_b, w_out):
    n = x1.shape[0]
    const = lambda i: (0, 0)
    return pl.pallas_call(
        _merge_kernel,
        out_shape=jax.ShapeDtypeStruct((n, D_MODEL), F32),
        grid=(n // MERGE_BM,),
        in_specs=[
            pl.BlockSpec((MERGE_BM, ATTN_WIDTH), lambda i: (i, 0)),
            pl.BlockSpec((MERGE_BM, SGU_WIDTH), lambda i: (i, 0)),
            pl.BlockSpec((MERGE_BM, 2 * D_MODEL), lambda i: (i, 0)),
            pl.BlockSpec((MERGE_BM, D_MODEL), lambda i: (i, 0)),
            pl.BlockSpec((ATTN_WIDTH, D_MODEL), const, pipeline_mode=pl.Buffered(1)),
            pl.BlockSpec((SGU_WIDTH, D_MODEL), const, pipeline_mode=pl.Buffered(1)),
            pl.BlockSpec((D_MODEL, D_MODEL), const, pipeline_mode=pl.Buffered(1)),
        ],
        out_specs=pl.BlockSpec((MERGE_BM, D_MODEL), lambda i: (i, 0)),
        scratch_shapes=[pltpu.VMEM((MERGE_BM, D_MODEL), BF16)],
        compiler_params=pltpu.CompilerParams(
            dimension_semantics=("arbitrary",),
            vmem_limit_bytes=VMEM_LIMIT),
        name="merge",
    )(attn, sgu, gates, x1, p_a, p_b, w_out)


def kernel(x, ffn1_norm, ffn1_w_gate, ffn1_w_up, ffn1_w_down, mix_norm, w_in, attn_sinks,
           sgu_norm_g, sgu_norm_b, sgu_w_s, sgu_b_s, w_proj_attn, w_proj_sgu,
           w_branch_gate, b_branch_gate, w_out, ffn2_norm, ffn2_w_gate, ffn2_w_up,
           ffn2_w_down, final_norm):
    batch, seq, d = x.shape
    assert (seq, d) == (SEQ, D_MODEL) and ffn1_norm.shape[0] == 1
    n = batch * seq
    xf = x.reshape(n, d)
    bf = lambda w: w.astype(BF16)
    row = lambda v: v.reshape(1, -1)

    x1, h2 = _ffn(xf, row(ffn1_norm[0]), bf(ffn1_w_gate[0]), bf(ffn1_w_up[0]), bf(ffn1_w_down[0]),
                  row(mix_norm[0]), final_norm=False)

    w_in_bf = bf(w_in[0])
    qkv, u, z = _proj(h2, w_in_bf[:, :QKV_WIDTH], w_in_bf[:, QKV_WIDTH:])

    bs_full = jnp.repeat(sgu_b_s[0].T, SGU_GROUP_DIM, axis=1)
    gates, attn, sgu = _gates_mixer(h2, bf(w_branch_gate[0]), row(b_branch_gate[0]), qkv, u, z,
                                    attn_sinks[0], sgu_w_s[0], bs_full,
                                    row(sgu_norm_g[0]), row(sgu_norm_b[0]))

    x2 = _merge(attn, sgu, gates, x1, bf(w_proj_attn[0]), bf(w_proj_sgu[0]), bf(w_out[0]))

    (out,) = _ffn(x2, row(ffn2_norm[0]), bf(ffn2_w_gate[0]), bf(ffn2_w_up[0]), bf(ffn2_w_down[0]),
                  row(final_norm), final_norm=True)
    return out.reshape(batch, seq, d)
```

```python
import functools
import math

import jax
import jax.numpy as jnp
from jax import lax
from jax.experimental import pallas as pl
from jax.experimental.pallas import tpu as pltpu

D_MODEL = 2048
SEQ = 2048
HEAD_DIM = 64
N_Q_HEADS = 16
N_KV_HEADS = 4
GQA_GROUP = N_Q_HEADS // N_KV_HEADS
BLOCK = 128
ATTN_WIDTH = N_Q_HEADS * HEAD_DIM
KV_WIDTH = N_KV_HEADS * HEAD_DIM
QKV_WIDTH = ATTN_WIDTH + 2 * KV_WIDTH
SGU_GROUPS = 16
SGU_WIDTH = 1024
SGU_GROUP_DIM = SGU_WIDTH // SGU_GROUPS
D_FF = 5632
EPS = 1e-6
NEG_INF = -1e30

BF16 = jnp.bfloat16
F32 = jnp.float32

FFN_BM = 1024
FFN_BF = 512
PROJ_BM = 1024
PROJ_BN = 512
GM_BM = 256
MERGE_BM = 512
MERGE_BN = 512
NORM_ROWS = 128
VMEM_LIMIT = 56 * 1024 * 1024
FFN_VMEM_LIMIT = 60 * 1024 * 1024

ALIBI_SLOPES = tuple(2.0 ** (-8.0 * (i + 1) / N_Q_HEADS) for i in range(N_Q_HEADS))


def _rms_scale(y, g):
    ms = jnp.mean(y * y, axis=-1, keepdims=True)
    return y * lax.rsqrt(ms + EPS) * g


N_FF_CHUNKS = D_FF // FFN_BF


BF16_TILE_ROWS = 16


def _cast_slab_rows(n_rows, total_chunks):
    rows = BF16_TILE_ROWS
    while n_rows // rows > total_chunks:
        rows *= 2
    assert n_rows % rows == 0
    return rows


def _ffn_kernel(*refs, final_norm, n_casts):
    n_out = 1 if final_norm else 2
    x_hbm, g_ref, wg_hbm, wu_hbm, wd_hbm, ng_ref = refs[:6]
    cast_src = refs[6:6 + n_casts]
    outs = refs[6 + n_casts:6 + n_casts + n_out + n_casts]
    o_ref = outs[0]
    hn_ref = None if final_norm else outs[1]
    cast_dst = outs[n_out:]
    scratch = refs[6 + n_casts + n_out + n_casts:]
    x_buf, h_ref, wg_buf, wu_buf, wd_buf, x_sem, w_sem = scratch[:7]
    cast_in = scratch[7:7 + n_casts]
    cast_out = scratch[7 + n_casts:7 + 2 * n_casts]
    if n_casts:
        cin_sem, cout_sem = scratch[7 + 2 * n_casts:]
    i = pl.program_id(0)
    n_blocks = pl.num_programs(0)

    def cast_in_copy(k, slab, slot):
        rows = cast_in[k].shape[1]
        span = pl.ds(pl.multiple_of(slab * rows, rows), rows)
        return pltpu.make_async_copy(cast_src[k].at[span, :], cast_in[k].at[slot], cin_sem.at[k, slot])

    def cast_out_copy(k, slab, slot):
        rows = cast_out[k].shape[1]
        span = pl.ds(pl.multiple_of(slab * rows, rows), rows)
        return pltpu.make_async_copy(cast_out[k].at[slot], cast_dst[k].at[span, :], cout_sem.at[k, slot])

    total_chunks = n_blocks * N_FF_CHUNKS

    def cast_slab(k, t):
        n_slabs = cast_src[k].shape[0] // cast_in[k].shape[1]
        assert n_slabs >= 2
        redo = n_slabs - 1 - lax.rem(t - (n_slabs - 1), 2)
        return jnp.where(t < n_slabs, t, redo)

    def when(cond, guarded):
        return pl.when(cond) if guarded else (lambda f: f())

    def cast_prime():
        for k in range(n_casts):
            cast_in_copy(k, 0, 0).start()
            cast_in_copy(k, 1, 1).start()
        for k in range(n_casts):
            cast_in_copy(k, 0, 0).wait()

    def cast_dma_tail(t, guarded):
        slot = lax.rem(t, 2)
        for k in range(n_casts):
            cast_out_copy(k, cast_slab(k, t), slot).start()

        @when(t >= 1, guarded)
        def _():
            for k in range(n_casts):
                cast_out_copy(k, cast_slab(k, t - 1), 1 - slot).wait()

        @when(t + 2 < total_chunks, guarded)
        def _():
            for k in range(n_casts):
                cast_in_copy(k, cast_slab(k, t + 2), slot).start()

        @when(t + 1 < total_chunks, guarded)
        def _():
            for k in range(n_casts):
                cast_in_copy(k, cast_slab(k, t + 1), 1 - slot).wait()

    def cast_drain():
        last = total_chunks - 1
        for k in range(n_casts):
            cast_out_copy(k, cast_slab(k, last), lax.rem(last, 2)).wait()

    def x_copy(block):
        rows = pl.ds(pl.multiple_of(block * FFN_BM, FFN_BM), FFN_BM)
        return pltpu.make_async_copy(x_hbm.at[rows, :], x_buf, x_sem.at[0])

    def w_copies(chunk, slot):
        span = pl.ds(pl.multiple_of(chunk * FFN_BF, FFN_BF), FFN_BF)
        return (
            pltpu.make_async_copy(wg_hbm.at[:, span], wg_buf.at[slot], w_sem.at[0, slot]),
            pltpu.make_async_copy(wu_hbm.at[:, span], wu_buf.at[slot], w_sem.at[1, slot]),
            pltpu.make_async_copy(wd_hbm.at[span, :], wd_buf.at[slot], w_sem.at[2, slot]),
        )

    first_slot = lax.rem(i * N_FF_CHUNKS, 2)

    @pl.when(i == 0)
    def _():
        x_copy(0).start()
        for cp in w_copies(0, 0):
            cp.start()
        cast_prime()

    x_copy(i).wait()

    def norm_body(r, carry):
        rows = pl.ds(pl.multiple_of(r * NORM_ROWS, NORM_ROWS), NORM_ROWS)
        xs = x_buf[rows, :]
        h_ref[rows, :] = _rms_scale(xs, g_ref[...]).astype(BF16)
        o_ref[rows, :] = xs
        return carry
    lax.fori_loop(0, FFN_BM // NORM_ROWS, norm_body, 0, unroll=True)

    @pl.when(i + 1 < n_blocks)
    def _():
        x_copy(i + 1).start()

    def chunk_body(c, carry, *, guarded):
        slot = lax.rem(first_slot + c, 2)
        for cp in w_copies(c, slot):
            cp.wait()

        @when(jnp.logical_or(c + 1 < N_FF_CHUNKS, i + 1 < n_blocks), guarded)
        def _():
            for cp in w_copies(lax.rem(c + 1, N_FF_CHUNKS), 1 - slot):
                cp.start()

        t = i * N_FF_CHUNKS + c
        cast_slot = lax.rem(t, 2)
        for k in range(n_casts):
            cast_out[k][cast_slot] = cast_in[k][cast_slot].astype(BF16)

        h = h_ref[...]
        gate = jnp.dot(h, wg_buf[slot], preferred_element_type=F32)
        up = jnp.dot(h, wu_buf[slot], preferred_element_type=F32)
        act = (gate * jax.nn.sigmoid(gate) * up * 0.5).astype(BF16)
        o_ref[...] += jnp.dot(act, wd_buf[slot], preferred_element_type=F32)

        if n_casts:
            cast_dma_tail(t, guarded)
        return carry

    is_edge_block = jnp.logical_or(i == 0, i == n_blocks - 1)

    @pl.when(is_edge_block)
    def _():
        lax.fori_loop(0, N_FF_CHUNKS, functools.partial(chunk_body, guarded=True), 0)

    @pl.when(jnp.logical_not(is_edge_block))
    def _():
        lax.fori_loop(0, N_FF_CHUNKS, functools.partial(chunk_body, guarded=False), 0)

    def final_body(r, carry):
        rows = pl.ds(pl.multiple_of(r * NORM_ROWS, NORM_ROWS), NORM_ROWS)
        normed = _rms_scale(o_ref[rows, :], ng_ref[...])
        if final_norm:
            o_ref[rows, :] = normed
        else:
            hn_ref[rows, :] = normed.astype(BF16)
        return carry
    lax.fori_loop(0, FFN_BM // NORM_ROWS, final_body, 0, unroll=True)

    if n_casts:
        @pl.when(i == n_blocks - 1)
        def _():
            cast_drain()


def _ffn(x, norm_g, w_gate, w_up, w_down, next_g, *, final_norm, casts=()):
    n = x.shape[0]
    n_blocks = n // FFN_BM
    row_block = lambda i: (i, 0)
    const = lambda i: (0, 0)
    hbm = pl.BlockSpec(memory_space=pl.ANY)
    out_shape = [jax.ShapeDtypeStruct((n, D_MODEL), F32)]
    out_specs = [pl.BlockSpec((FFN_BM, D_MODEL), row_block)]
    if not final_norm:
        out_shape.append(jax.ShapeDtypeStruct((n, D_MODEL), BF16))
        out_specs.append(pl.BlockSpec((FFN_BM, D_MODEL), row_block))
    scratch = [
        pltpu.VMEM((FFN_BM, D_MODEL), F32),
        pltpu.VMEM((FFN_BM, D_MODEL), BF16),
        pltpu.VMEM((2, D_MODEL, FFN_BF), BF16),
        pltpu.VMEM((2, D_MODEL, FFN_BF), BF16),
        pltpu.VMEM((2, FFN_BF, D_MODEL), BF16),
        pltpu.SemaphoreType.DMA((1,)),
        pltpu.SemaphoreType.DMA((3, 2)),
    ]
    if casts:
        slab_rows = [_cast_slab_rows(w.shape[0], n_blocks * N_FF_CHUNKS) for w in casts]
        out_shape += [jax.ShapeDtypeStruct(w.shape, BF16) for w in casts]
        out_specs += [hbm] * len(casts)
        scratch += [pltpu.VMEM((2, r, w.shape[1]), F32) for r, w in zip(slab_rows, casts)]
        scratch += [pltpu.VMEM((2, r, w.shape[1]), BF16) for r, w in zip(slab_rows, casts)]
        scratch += [pltpu.SemaphoreType.DMA((len(casts), 2))] * 2
    return pl.pallas_call(
        functools.partial(_ffn_kernel, final_norm=final_norm, n_casts=len(casts)),
        out_shape=tuple(out_shape),
        grid=(n_blocks,),
        in_specs=[hbm, pl.BlockSpec((1, D_MODEL), const), hbm, hbm, hbm,
                  pl.BlockSpec((1, D_MODEL), const)] + [hbm] * len(casts),
        out_specs=tuple(out_specs),
        scratch_shapes=scratch,
        compiler_params=pltpu.CompilerParams(
            dimension_semantics=("arbitrary",),
            vmem_limit_bytes=FFN_VMEM_LIMIT),
        name="ffn_final" if final_norm else "ffn",
    )(x, norm_g, w_gate, w_up, w_down, next_g, *casts)


def _gelu_tanh(x):
    c = math.sqrt(2.0 / math.pi)
    return 0.5 * x * (1.0 + jnp.tanh(c * (x + 0.044715 * (x * x * x))))


def _proj_kernel(h_ref, w_ref, qkv_ref, u_ref, z_ref):
    h = h_ref[...]
    for c in range(QKV_WIDTH // PROJ_BN):
        cols = slice(c * PROJ_BN, (c + 1) * PROJ_BN)
        acc = jnp.dot(h, w_ref[:, cols], preferred_element_type=F32)
        if (c + 1) * PROJ_BN <= ATTN_WIDTH:
            acc = acc * (HEAD_DIM ** -0.5)
        qkv_ref[:, cols] = acc.astype(BF16)
    for c in range(2 * SGU_WIDTH // PROJ_BN):
        w_cols = slice(QKV_WIDTH + c * PROJ_BN, QKV_WIDTH + (c + 1) * PROJ_BN)
        acc = jnp.dot(h, w_ref[:, w_cols], preferred_element_type=F32)
        dst = u_ref if c * PROJ_BN < SGU_WIDTH else z_ref
        off = (c * PROJ_BN) % SGU_WIDTH
        dst[:, off:off + PROJ_BN] = _gelu_tanh(acc).astype(BF16)


def _proj(h2, w_in):
    n = h2.shape[0]
    row_block = lambda i: (i, 0)
    const = lambda i: (0, 0)
    return pl.pallas_call(
        _proj_kernel,
        out_shape=(
            jax.ShapeDtypeStruct((n, QKV_WIDTH), BF16),
            jax.ShapeDtypeStruct((n, SGU_WIDTH), BF16),
            jax.ShapeDtypeStruct((n, SGU_WIDTH), BF16),
        ),
        grid=(n // PROJ_BM,),
        in_specs=[
            pl.BlockSpec((PROJ_BM, D_MODEL), row_block),
            pl.BlockSpec((D_MODEL, QKV_WIDTH + 2 * SGU_WIDTH), const, pipeline_mode=pl.Buffered(1)),
        ],
        out_specs=(
            pl.BlockSpec((PROJ_BM, QKV_WIDTH), row_block),
            pl.BlockSpec((PROJ_BM, SGU_WIDTH), row_block),
            pl.BlockSpec((PROJ_BM, SGU_WIDTH), row_block),
        ),
        compiler_params=pltpu.CompilerParams(
            dimension_semantics=("arbitrary",),
            vmem_limit_bytes=VMEM_LIMIT),
        name="proj",
    )(h2, w_in)


GM_SUB = GM_BM // BLOCK
GM_GATE_CHUNKS = 2 * D_MODEL // PROJ_BN
K_OFF = ATTN_WIDTH


def _mix_block(r, no_prev, sinks_ref, qkv_ref, u_ref, z_ref, bs_ref, lng_ref, lnb_ref,
               attn_ref, sgu_ref, kv_buf, ws_buf, bias_buf):
    rows = slice(r * BLOCK, (r + 1) * BLOCK)
    band = slice(r * BLOCK, (r + 2) * BLOCK)

    outs = []
    for hk in range(N_KV_HEADS):
        k_band = kv_buf[band, hk * HEAD_DIM:(hk + 1) * HEAD_DIM]
        v_band = kv_buf[band, KV_WIDTH + hk * HEAD_DIM:KV_WIDTH + (hk + 1) * HEAD_DIM]
        for gq in range(GQA_GROUP):
            h = hk * GQA_GROUP + gq
            q_h = qkv_ref[rows, h * HEAD_DIM:(h + 1) * HEAD_DIM]
            s = lax.dot_general(q_h, k_band, (((1,), (1,)), ((), ())),
                                preferred_element_type=F32)
            s = s + bias_buf[no_prev, h]
            sink = sinks_ref[h]
            m = jnp.maximum(jnp.max(s, axis=-1, keepdims=True), sink)
            p = jnp.exp(s - m)
            denom = jnp.sum(p, axis=-1, keepdims=True) + jnp.exp(sink - m)
            o = jnp.dot(p.astype(BF16), v_band, preferred_element_type=F32)
            outs.append(o / denom)
    attn_ref[rows, :] = jnp.concatenate(outs, axis=-1).astype(BF16)

    zc = z_ref[rows, :].astype(F32)
    mu = jnp.mean(zc, axis=-1, keepdims=True)
    zc = zc - mu
    var = jnp.mean(zc * zc, axis=-1, keepdims=True)
    zl = (zc * lax.rsqrt(var + EPS) * lng_ref[...] + lnb_ref[...]).astype(BF16)
    mixed = [
        jnp.dot(ws_buf[g], zl[:, g * SGU_GROUP_DIM:(g + 1) * SGU_GROUP_DIM],
                preferred_element_type=F32)
        for g in range(SGU_GROUPS)
    ]
    mixed = jnp.concatenate(mixed, axis=-1) + bs_ref[...]
    sgu_ref[rows, :] = (u_ref[rows, :].astype(F32) * mixed).astype(BF16)


def _gates_mixer_kernel(sinks_ref, h_ref, wg_ref, bg_ref, qkv_ref, halo_ref, u_ref, z_ref, ws_ref,
                        bs_ref, lng_ref, lnb_ref, gt_ref, attn_ref, sgu_ref,
                        kv_buf, ws_buf, bias_buf):
    i = pl.program_id(0)

    @pl.when(i == 0)
    def _():
        tq = lax.broadcasted_iota(jnp.int32, (BLOCK, BLOCK), 0)
        ts = lax.broadcasted_iota(jnp.int32, (BLOCK, BLOCK), 1)
        causal = ts <= tq
        for g in range(SGU_GROUPS):
            ws_buf[g] = jnp.where(causal, ws_ref[g], 0.0).astype(BF16)

        qpos = lax.broadcasted_iota(jnp.int32, (BLOCK, 2 * BLOCK), 0) + BLOCK
        kpos = lax.broadcasted_iota(jnp.int32, (BLOCK, 2 * BLOCK), 1)
        dist_i = qpos - kpos
        in_window = (dist_i >= 0) & (dist_i < BLOCK)
        dist = jnp.abs(dist_i).astype(F32)
        for h in range(N_Q_HEADS):
            alibi = -ALIBI_SLOPES[h] * dist
            bias_buf[0, h] = jnp.where(in_window, alibi, NEG_INF)
            bias_buf[1, h] = jnp.where(in_window & (kpos >= BLOCK), alibi, NEG_INF)

    kv_buf[0:BLOCK, :] = halo_ref[...]
    kv_buf[BLOCK:, :] = qkv_ref[:, K_OFF:]

    step_starts_sequence = ((i % (SEQ // GM_BM)) == 0).astype(jnp.int32)

    h = h_ref[...]
    chunks_per_block = GM_GATE_CHUNKS // GM_SUB
    for c in range(GM_GATE_CHUNKS):
        cols = slice(c * PROJ_BN, (c + 1) * PROJ_BN)
        acc = jnp.dot(h, wg_ref[:, cols], preferred_element_type=F32)
        gt_ref[:, cols] = jax.nn.sigmoid(acc + bg_ref[:, cols]).astype(BF16)
        if c % chunks_per_block == 0:
            r = c // chunks_per_block
            _mix_block(r, step_starts_sequence if r == 0 else 0,
                       sinks_ref, qkv_ref, u_ref, z_ref, bs_ref, lng_ref, lnb_ref,
                       attn_ref, sgu_ref, kv_buf, ws_buf, bias_buf)


def _gates_mixer(h2, w_gate, b_gate, qkv, u, z, sinks, ws, bs_full, ln_g, ln_b):
    n = h2.shape[0]
    row_block = lambda i: (i, 0)
    const = lambda i: (0, 0)
    return pl.pallas_call(
        _gates_mixer_kernel,
        out_shape=(
            jax.ShapeDtypeStruct((n, 2 * D_MODEL), BF16),
            jax.ShapeDtypeStruct((n, ATTN_WIDTH), BF16),
            jax.ShapeDtypeStruct((n, SGU_WIDTH), BF16),
        ),
        grid=(n // GM_BM,),
        in_specs=[
            pl.BlockSpec(memory_space=pltpu.SMEM),
            pl.BlockSpec((GM_BM, D_MODEL), row_block),
            pl.BlockSpec((D_MODEL, 2 * D_MODEL), const, pipeline_mode=pl.Buffered(1)),
            pl.BlockSpec((1, 2 * D_MODEL), const),
            pl.BlockSpec((GM_BM, QKV_WIDTH), row_block),
            pl.BlockSpec((BLOCK, 2 * KV_WIDTH),
                         lambda i: (jnp.maximum(i * GM_SUB - 1, 0), ATTN_WIDTH // (2 * KV_WIDTH))),
            pl.BlockSpec((GM_BM, SGU_WIDTH), row_block),
            pl.BlockSpec((GM_BM, SGU_WIDTH), row_block),
            pl.BlockSpec((SGU_GROUPS, BLOCK, BLOCK), lambda i: (0, 0, 0)),
            pl.BlockSpec((BLOCK, SGU_WIDTH), const),
            pl.BlockSpec((1, SGU_WIDTH), const),
            pl.BlockSpec((1, SGU_WIDTH), const),
        ],
        out_specs=(
            pl.BlockSpec((GM_BM, 2 * D_MODEL), row_block),
            pl.BlockSpec((GM_BM, ATTN_WIDTH), row_block),
            pl.BlockSpec((GM_BM, SGU_WIDTH), row_block),
        ),
        scratch_shapes=[
            pltpu.VMEM((GM_BM + BLOCK, 2 * KV_WIDTH), BF16),
            pltpu.VMEM((SGU_GROUPS, BLOCK, BLOCK), BF16),
            pltpu.VMEM((2, N_Q_HEADS, BLOCK, 2 * BLOCK), F32),
        ],
        compiler_params=pltpu.CompilerParams(
            dimension_semantics=("arbitrary",),
            vmem_limit_bytes=VMEM_LIMIT),
        name="gates_mixer",
    )(sinks, h2, w_gate, b_gate, qkv, qkv, u, z, ws, bs_full, ln_g, ln_b)


def _merge_kernel(attn_ref, sgu_ref, gt_ref, x_ref, pa_ref, pb_ref, wo_ref, o_ref, m_ref):
    attn = attn_ref[...]
    sgu = sgu_ref[...]
    for c in range(D_MODEL // MERGE_BN):
        cols = slice(c * MERGE_BN, (c + 1) * MERGE_BN)
        a = jnp.dot(attn, pa_ref[:, cols], preferred_element_type=F32)
        b = jnp.dot(sgu, pb_ref[:, cols], preferred_element_type=F32)
        g_a = gt_ref[:, cols].astype(F32)
        g_b = gt_ref[:, D_MODEL + c * MERGE_BN:D_MODEL + (c + 1) * MERGE_BN].astype(F32)
        m_ref[:, cols] = (g_a * a + g_b * b).astype(BF16)
    o_ref[...] = x_ref[...] + jnp.dot(m_ref[...], wo_ref[...], preferred_element_type=F32)


def _merge(attn, sgu, gates, x1, p_a, p_b, w_out):
    n = x1.shape[0]
    const = lambda i: (0, 0)
    return pl.pallas_call(
        _merge_kernel,
        out_shape=jax.ShapeDtypeStruct((n, D_MODEL), F32),
        grid=(n // MERGE_BM,),
        in_specs=[
            pl.BlockSpec((MERGE_BM, ATTN_WIDTH), lambda i: (i, 0)),
            pl.BlockSpec((MERGE_BM, SGU_WIDTH), lambda i: (i, 0)),
            pl.BlockSpec((MERGE_BM, 2 * D_MODEL), lambda i: (i, 0)),
            pl.BlockSpec((MERGE_BM, D_MODEL), lambda i: (i, 0)),
            pl.BlockSpec((ATTN_WIDTH, D_MODEL), const, pipeline_mode=pl.Buffered(1)),
            pl.BlockSpec((SGU_WIDTH, D_MODEL), const, pipeline_mode=pl.Buffered(1)),
            pl.BlockSpec((D_MODEL, D_MODEL), const, pipeline_mode=pl.Buffered(1)),
        ],
        out_specs=pl.BlockSpec((MERGE_BM, D_MODEL), lambda i: (i, 0)),
        scratch_shapes=[pltpu.VMEM((MERGE_BM, D_MODEL), BF16)],
        compiler_params=pltpu.CompilerParams(
            dimension_semantics=("arbitrary",),
            vmem_limit_bytes=VMEM_LIMIT),
        name="merge",
    )(attn, sgu, gates, x1, p_a, p_b, w_out)


def kernel(x, ffn1_norm, ffn1_w_gate, ffn1_w_up, ffn1_w_down, mix_norm, w_in, attn_sinks,
           sgu_norm_g, sgu_norm_b, sgu_w_s, sgu_b_s, w_proj_attn, w_proj_sgu,
           w_branch_gate, b_branch_gate, w_out, ffn2_norm, ffn2_w_gate, ffn2_w_up,
           ffn2_w_down, final_norm):
    batch, seq, d = x.shape
    assert (seq, d) == (SEQ, D_MODEL) and ffn1_norm.shape[0] == 1
    n = batch * seq
    xf = x.reshape(n, d)
    bf = lambda w: w.astype(BF16)
    row = lambda v: v.reshape(1, -1)

    later_weights = (w_in[0], w_branch_gate[0], w_proj_attn[0], w_proj_sgu[0], w_out[0],
                     ffn2_w_gate[0], ffn2_w_up[0], ffn2_w_down[0])
    (x1, h2, w_in_bf, w_bgate_bf, p_a_bf, p_b_bf, w_out_bf, f2_gate_bf, f2_up_bf, f2_down_bf) = _ffn(
        xf, row(ffn1_norm[0]), bf(ffn1_w_gate[0]), bf(ffn1_w_up[0]), bf(ffn1_w_down[0]),
        row(mix_norm[0]), final_norm=False, casts=later_weights)

    qkv, u, z = _proj(h2, w_in_bf)

    bs_full = jnp.repeat(sgu_b_s[0].T, SGU_GROUP_DIM, axis=1)
    gates, attn, sgu = _gates_mixer(h2, w_bgate_bf, row(b_branch_gate[0]), qkv, u, z,
                                    attn_sinks[0], sgu_w_s[0], bs_full,
                                    row(sgu_norm_g[0]), row(sgu_norm_b[0]))

    x2 = _merge(attn, sgu, gates, x1, p_a_bf, p_b_bf, w_out_bf)

    (out,) = _ffn(x2, row(ffn2_norm[0]), f2_gate_bf, f2_up_bf, f2_down_bf,
                  row(final_norm), final_norm=True)
    return out.reshape(batch, seq, d)
```

```python
import functools
import math

import jax
import jax.numpy as jnp
from jax import lax
from jax.experimental import pallas as pl
from jax.experimental.pallas import tpu as pltpu

D_MODEL = 2048
SEQ = 2048
HEAD_DIM = 64
N_Q_HEADS = 16
N_KV_HEADS = 4
GQA_GROUP = N_Q_HEADS // N_KV_HEADS
BLOCK = 128
ATTN_WIDTH = N_Q_HEADS * HEAD_DIM
KV_WIDTH = N_KV_HEADS * HEAD_DIM
QKV_WIDTH = ATTN_WIDTH + 2 * KV_WIDTH
SGU_GROUPS = 16
SGU_WIDTH = 1024
SGU_GROUP_DIM = SGU_WIDTH // SGU_GROUPS
D_FF = 5632
EPS = 1e-6
NEG_INF = -1e30

BF16 = jnp.bfloat16
F32 = jnp.float32

FFN_BM = 1024
FFN_BF = 512
PROJ_BM = 1024
PROJ_BN = 512
GM_BM = 256
MERGE_BM = 512
MERGE_BN = 512
NORM_ROWS = 128
VMEM_LIMIT = 56 * 1024 * 1024
FFN_VMEM_LIMIT = 60 * 1024 * 1024

ALIBI_SLOPES = tuple(2.0 ** (-8.0 * (i + 1) / N_Q_HEADS) for i in range(N_Q_HEADS))


def _rms_scale(y, g):
    ms = jnp.mean(y * y, axis=-1, keepdims=True)
    return y * lax.rsqrt(ms + EPS) * g


N_FF_CHUNKS = D_FF // FFN_BF


BF16_TILE_ROWS = 16


def _cast_slab_rows(n_rows, total_chunks):
    rows = BF16_TILE_ROWS
    while n_rows // rows > total_chunks:
        rows *= 2
    assert n_rows % rows == 0
    return rows


def _ffn_kernel(*refs, final_norm, n_casts):
    n_out = 1 if final_norm else 2
    x_hbm, g_ref, wg_hbm, wu_hbm, wd_hbm, ng_ref = refs[:6]
    cast_src = refs[6:6 + n_casts]
    outs = refs[6 + n_casts:6 + n_casts + n_out + n_casts]
    o_ref = outs[0]
    hn_ref = None if final_norm else outs[1]
    cast_dst = outs[n_out:]
    scratch = refs[6 + n_casts + n_out + n_casts:]
    x_buf, h_ref, wg_buf, wu_buf, wd_buf, x_sem, w_sem = scratch[:7]
    cast_in = scratch[7:7 + n_casts]
    cast_out = scratch[7 + n_casts:7 + 2 * n_casts]
    if n_casts:
        cin_sem, cout_sem = scratch[7 + 2 * n_casts:]
    i = pl.program_id(0)
    n_blocks = pl.num_programs(0)

    def cast_in_copy(k, slab, slot):
        rows = cast_in[k].shape[1]
        span = pl.ds(pl.multiple_of(slab * rows, rows), rows)
        return pltpu.make_async_copy(cast_src[k].at[span, :], cast_in[k].at[slot], cin_sem.at[k, slot])

    def cast_out_copy(k, slab, slot):
        rows = cast_out[k].shape[1]
        span = pl.ds(pl.multiple_of(slab * rows, rows), rows)
        return pltpu.make_async_copy(cast_out[k].at[slot], cast_dst[k].at[span, :], cout_sem.at[k, slot])

    total_chunks = n_blocks * N_FF_CHUNKS

    def cast_slab(k, t):
        n_slabs = cast_src[k].shape[0] // cast_in[k].shape[1]
        assert n_slabs >= 2
        redo = n_slabs - 1 - lax.rem(t - (n_slabs - 1), 2)
        return jnp.where(t < n_slabs, t, redo)

    def when(cond, guarded):
        return pl.when(cond) if guarded else (lambda f: f())

    def cast_prime():
        for k in range(n_casts):
            cast_in_copy(k, 0, 0).start()
            cast_in_copy(k, 1, 1).start()
        for k in range(n_casts):
            cast_in_copy(k, 0, 0).wait()

    def cast_dma_tail(t, guarded):
        slot = lax.rem(t, 2)
        for k in range(n_casts):
            cast_out_copy(k, cast_slab(k, t), slot).start()

        @when(t >= 1, guarded)
        def _():
            for k in range(n_casts):
                cast_out_copy(k, cast_slab(k, t - 1), 1 - slot).wait()

        @when(t + 2 < total_chunks, guarded)
        def _():
            for k in range(n_casts):
                cast_in_copy(k, cast_slab(k, t + 2), slot).start()

        @when(t + 1 < total_chunks, guarded)
        def _():
            for k in range(n_casts):
                cast_in_copy(k, cast_slab(k, t + 1), 1 - slot).wait()

    def cast_drain():
        last = total_chunks - 1
        for k in range(n_casts):
            cast_out_copy(k, cast_slab(k, last), lax.rem(last, 2)).wait()

    def x_copy(block):
        rows = pl.ds(pl.multiple_of(block * FFN_BM, FFN_BM), FFN_BM)
        return pltpu.make_async_copy(x_hbm.at[rows, :], x_buf, x_sem.at[0])

    def w_copies(chunk, slot):
        span = pl.ds(pl.multiple_of(chunk * FFN_BF, FFN_BF), FFN_BF)
        return (
            pltpu.make_async_copy(wg_hbm.at[:, span], wg_buf.at[slot], w_sem.at[0, slot]),
            pltpu.make_async_copy(wu_hbm.at[:, span], wu_buf.at[slot], w_sem.at[1, slot]),
            pltpu.make_async_copy(wd_hbm.at[span, :], wd_buf.at[slot], w_sem.at[2, slot]),
        )

    first_slot = lax.rem(i * N_FF_CHUNKS, 2)

    @pl.when(i == 0)
    def _():
        x_copy(0).start()
        for cp in w_copies(0, 0):
            cp.start()
        cast_prime()

    x_copy(i).wait()

    def norm_body(r, carry):
        rows = pl.ds(pl.multiple_of(r * NORM_ROWS, NORM_ROWS), NORM_ROWS)
        xs = x_buf[rows, :]
        h_ref[rows, :] = _rms_scale(xs, g_ref[...]).astype(BF16)
        o_ref[rows, :] = xs
        return carry
    lax.fori_loop(0, FFN_BM // NORM_ROWS, norm_body, 0, unroll=True)

    @pl.when(i + 1 < n_blocks)
    def _():
        x_copy(i + 1).start()

    def chunk_body(c, carry, *, guarded):
        slot = lax.rem(first_slot + c, 2)
        for cp in w_copies(c, slot):
            cp.wait()

        @when(jnp.logical_or(c + 1 < N_FF_CHUNKS, i + 1 < n_blocks), guarded)
        def _():
            for cp in w_copies(lax.rem(c + 1, N_FF_CHUNKS), 1 - slot):
                cp.start()

        t = i * N_FF_CHUNKS + c
        cast_slot = lax.rem(t, 2)
        for k in range(n_casts):
            cast_out[k][cast_slot] = cast_in[k][cast_slot].astype(BF16)

        h = h_ref[...]
        gate = jnp.dot(h, wg_buf[slot], preferred_element_type=F32)
        up = jnp.dot(h, wu_buf[slot], preferred_element_type=F32)
        act = (gate * jax.nn.sigmoid(gate) * up * 0.5).astype(BF16)
        o_ref[...] += jnp.dot(act, wd_buf[slot], preferred_element_type=F32)

        if n_casts:
            cast_dma_tail(t, guarded)
        return carry

    is_edge_block = jnp.logical_or(i == 0, i == n_blocks - 1)

    @pl.when(is_edge_block)
    def _():
        lax.fori_loop(0, N_FF_CHUNKS, functools.partial(chunk_body, guarded=True), 0)

    @pl.when(jnp.logical_not(is_edge_block))
    def _():
        lax.fori_loop(0, N_FF_CHUNKS, functools.partial(chunk_body, guarded=False), 0)

    def final_body(r, carry):
        rows = pl.ds(pl.multiple_of(r * NORM_ROWS, NORM_ROWS), NORM_ROWS)
        normed = _rms_scale(o_ref[rows, :], ng_ref[...])
        if final_norm:
            o_ref[rows, :] = normed
        else:
            hn_ref[rows, :] = normed.astype(BF16)
        return carry
    lax.fori_loop(0, FFN_BM // NORM_ROWS, final_body, 0, unroll=True)

    if n_casts:
        @pl.when(i == n_blocks - 1)
        def _():
            cast_drain()


def _ffn(x, norm_g, w_gate, w_up, w_down, next_g, *, final_norm, casts=()):
    n = x.shape[0]
    n_blocks = n // FFN_BM
    row_block = lambda i: (i, 0)
    const = lambda i: (0, 0)
    hbm = pl.BlockSpec(memory_space=pl.ANY)
    out_shape = [jax.ShapeDtypeStruct((n, D_MODEL), F32)]
    out_specs = [pl.BlockSpec((FFN_BM, D_MODEL), row_block)]
    if not final_norm:
        out_shape.append(jax.ShapeDtypeStruct((n, D_MODEL), BF16))
        out_specs.append(pl.BlockSpec((FFN_BM, D_MODEL), row_block))
    scratch = [
        pltpu.VMEM((FFN_BM, D_MODEL), F32),
        pltpu.VMEM((FFN_BM, D_MODEL), BF16),
        pltpu.VMEM((2, D_MODEL, FFN_BF), BF16),
        pltpu.VMEM((2, D_MODEL, FFN_BF), BF16),
        pltpu.VMEM((2, FFN_BF, D_MODEL), BF16),
        pltpu.SemaphoreType.DMA((1,)),
        pltpu.SemaphoreType.DMA((3, 2)),
    ]
    if casts:
        slab_rows = [_cast_slab_rows(w.shape[0], n_blocks * N_FF_CHUNKS) for w in casts]
        out_shape += [jax.ShapeDtypeStruct(w.shape, BF16) for w in casts]
        out_specs += [hbm] * len(casts)
        scratch += [pltpu.VMEM((2, r, w.shape[1]), F32) for r, w in zip(slab_rows, casts)]
        scratch += [pltpu.VMEM((2, r, w.shape[1]), BF16) for r, w in zip(slab_rows, casts)]
        scratch += [pltpu.SemaphoreType.DMA((len(casts), 2))] * 2
    return pl.pallas_call(
        functools.partial(_ffn_kernel, final_norm=final_norm, n_casts=len(casts)),
        out_shape=tuple(out_shape),
        grid=(n_blocks,),
        in_specs=[hbm, pl.BlockSpec((1, D_MODEL), const), hbm, hbm, hbm,
                  pl.BlockSpec((1, D_MODEL), const)] + [hbm] * len(casts),
        out_specs=tuple(out_specs),
        scratch_shapes=scratch,
        compiler_params=pltpu.CompilerParams(
            dimension_semantics=("arbitrary",),
            vmem_limit_bytes=FFN_VMEM_LIMIT),
        name="ffn_final" if final_norm else "ffn",
    )(x, norm_g, w_gate, w_up, w_down, next_g, *casts)


def _gelu_tanh(x):
    c = math.sqrt(2.0 / math.pi)
    return 0.5 * x * (1.0 + jnp.tanh(c * (x + 0.044715 * (x * x * x))))


def _proj_kernel(h_ref, w_ref, qkv_ref, u_ref, z_ref):
    h = h_ref[...]
    for c in range(QKV_WIDTH // PROJ_BN):
        cols = slice(c * PROJ_BN, (c + 1) * PROJ_BN)
        acc = jnp.dot(h, w_ref[:, cols], preferred_element_type=F32)
        if (c + 1) * PROJ_BN <= ATTN_WIDTH:
            acc = acc * (HEAD_DIM ** -0.5)
        qkv_ref[:, cols] = acc.astype(BF16)
    for c in range(2 * SGU_WIDTH // PROJ_BN):
        w_cols = slice(QKV_WIDTH + c * PROJ_BN, QKV_WIDTH + (c + 1) * PROJ_BN)
        acc = jnp.dot(h, w_ref[:, w_cols], preferred_element_type=F32)
        dst = u_ref if c * PROJ_BN < SGU_WIDTH else z_ref
        off = (c * PROJ_BN) % SGU_WIDTH
        dst[:, off:off + PROJ_BN] = _gelu_tanh(acc).astype(BF16)


def _proj(h2, w_in):
    n = h2.shape[0]
    row_block = lambda i: (i, 0)
    const = lambda i: (0, 0)
    return pl.pallas_call(
        _proj_kernel,
        out_shape=(
            jax.ShapeDtypeStruct((n, QKV_WIDTH), BF16),
            jax.ShapeDtypeStruct((n, SGU_WIDTH), BF16),
            jax.ShapeDtypeStruct((n, SGU_WIDTH), BF16),
        ),
        grid=(n // PROJ_BM,),
        in_specs=[
            pl.BlockSpec((PROJ_BM, D_MODEL), row_block),
            pl.BlockSpec((D_MODEL, QKV_WIDTH + 2 * SGU_WIDTH), const, pipeline_mode=pl.Buffered(1)),
        ],
        out_specs=(
            pl.BlockSpec((PROJ_BM, QKV_WIDTH), row_block),
            pl.BlockSpec((PROJ_BM, SGU_WIDTH), row_block),
            pl.BlockSpec((PROJ_BM, SGU_WIDTH), row_block),
        ),
        compiler_params=pltpu.CompilerParams(
            dimension_semantics=("arbitrary",),
            vmem_limit_bytes=VMEM_LIMIT),
        name="proj",
    )(h2, w_in)


GM_SUB = GM_BM // BLOCK
GM_GATE_CHUNKS = 2 * D_MODEL // PROJ_BN
K_OFF = ATTN_WIDTH


def _mix_pieces(r, no_prev, sinks_ref, qkv_ref, u_ref, z_ref, bs_ref, lng_ref, lnb_ref,
                attn_ref, sgu_ref, kv_buf, ws_buf, bias_buf):
    rows = slice(r * BLOCK, (r + 1) * BLOCK)
    band = slice(r * BLOCK, (r + 2) * BLOCK)
    group_width = GQA_GROUP * HEAD_DIM

    def attention(hk):
        k_band = kv_buf[band, hk * HEAD_DIM:(hk + 1) * HEAD_DIM]
        v_band = kv_buf[band, KV_WIDTH + hk * HEAD_DIM:KV_WIDTH + (hk + 1) * HEAD_DIM]
        heads = range(hk * GQA_GROUP, (hk + 1) * GQA_GROUP)
        q = jnp.concatenate([qkv_ref[rows, h * HEAD_DIM:(h + 1) * HEAD_DIM] for h in heads], axis=0)
        s_all = lax.dot_general(q, k_band, (((1,), (1,)), ((), ())), preferred_element_type=F32)
        probs, denoms = [], []
        for gq, h in enumerate(heads):
            s = s_all[gq * BLOCK:(gq + 1) * BLOCK] + bias_buf[no_prev, h]
            sink = sinks_ref[h]
            m = jnp.maximum(jnp.max(s, axis=-1, keepdims=True), sink)
            p = jnp.exp(s - m)
            denoms.append(jnp.sum(p, axis=-1, keepdims=True) + jnp.exp(sink - m))
            probs.append(p.astype(BF16))
        o_all = jnp.dot(jnp.concatenate(probs, axis=0), v_band, preferred_element_type=F32)
        outs = [o_all[gq * BLOCK:(gq + 1) * BLOCK] / denoms[gq] for gq in range(GQA_GROUP)]
        attn_ref[rows, hk * group_width:(hk + 1) * group_width] = (
            jnp.concatenate(outs, axis=-1).astype(BF16))

    def gating():
        zc = z_ref[rows, :].astype(F32)
        mu = jnp.mean(zc, axis=-1, keepdims=True)
        zc = zc - mu
        var = jnp.mean(zc * zc, axis=-1, keepdims=True)
        zl = (zc * lax.rsqrt(var + EPS) * lng_ref[...] + lnb_ref[...]).astype(BF16)
        first_group = lax.broadcasted_iota(jnp.int32, (BLOCK, 2 * SGU_GROUP_DIM), 1) < SGU_GROUP_DIM
        zero = jnp.zeros((BLOCK, 2 * SGU_GROUP_DIM), BF16)
        mixed = []
        for j in range(SGU_GROUPS // 2):
            z_pair = zl[:, j * 2 * SGU_GROUP_DIM:(j + 1) * 2 * SGU_GROUP_DIM]
            z_diag = jnp.concatenate([jnp.where(first_group, z_pair, zero),
                                      jnp.where(first_group, zero, z_pair)], axis=0)
            mixed.append(jnp.dot(ws_buf[j], z_diag, preferred_element_type=F32))
        mixed = jnp.concatenate(mixed, axis=-1) + bs_ref[...]
        sgu_ref[rows, :] = (u_ref[rows, :].astype(F32) * mixed).astype(BF16)

    return [functools.partial(attention, hk) for hk in range(N_KV_HEADS)] + [gating]


def _gates_mixer_kernel(sinks_ref, h_ref, wg_ref, bg_ref, qkv_ref, halo_ref, u_ref, z_ref, ws_ref,
                        bs_ref, lng_ref, lnb_ref, gt_ref, attn_ref, sgu_ref,
                        kv_buf, ws_buf, bias_buf):
    i = pl.program_id(0)

    @pl.when(i == 0)
    def _():
        tq = lax.broadcasted_iota(jnp.int32, (BLOCK, BLOCK), 0)
        ts = lax.broadcasted_iota(jnp.int32, (BLOCK, BLOCK), 1)
        causal = ts <= tq
        for g in range(SGU_GROUPS):
            ws_buf[g // 2, :, (g % 2) * BLOCK:(g % 2 + 1) * BLOCK] = (
                jnp.where(causal, ws_ref[g], 0.0).astype(BF16))

        qpos = lax.broadcasted_iota(jnp.int32, (BLOCK, 2 * BLOCK), 0) + BLOCK
        kpos = lax.broadcasted_iota(jnp.int32, (BLOCK, 2 * BLOCK), 1)
        dist_i = qpos - kpos
        in_window = (dist_i >= 0) & (dist_i < BLOCK)
        dist = jnp.abs(dist_i).astype(F32)
        for h in range(N_Q_HEADS):
            alibi = -ALIBI_SLOPES[h] * dist
            bias_buf[0, h] = jnp.where(in_window, alibi, NEG_INF)
            bias_buf[1, h] = jnp.where(in_window & (kpos >= BLOCK), alibi, NEG_INF)

    kv_buf[0:BLOCK, :] = halo_ref[...]
    kv_buf[BLOCK:, :] = qkv_ref[:, K_OFF:]

    step_starts_sequence = ((i % (SEQ // GM_BM)) == 0).astype(jnp.int32)

    pieces = []
    for r in range(GM_SUB):
        pieces += _mix_pieces(r, step_starts_sequence if r == 0 else 0,
                              sinks_ref, qkv_ref, u_ref, z_ref, bs_ref, lng_ref, lnb_ref,
                              attn_ref, sgu_ref, kv_buf, ws_buf, bias_buf)

    h = h_ref[...]
    for c in range(GM_GATE_CHUNKS):
        cols = slice(c * PROJ_BN, (c + 1) * PROJ_BN)
        acc = jnp.dot(h, wg_ref[:, cols], preferred_element_type=F32)
        gt_ref[:, cols] = jax.nn.sigmoid(acc + bg_ref[:, cols]).astype(BF16)
        for piece in pieces[c * len(pieces) // GM_GATE_CHUNKS:(c + 1) * len(pieces) // GM_GATE_CHUNKS]:
            piece()


def _gates_mixer(h2, w_gate, b_gate, qkv, u, z, sinks, ws, bs_full, ln_g, ln_b):
    n = h2.shape[0]
    row_block = lambda i: (i, 0)
    const = lambda i: (0, 0)
    return pl.pallas_call(
        _gates_mixer_kernel,
        out_shape=(
            jax.ShapeDtypeStruct((n, 2 * D_MODEL), BF16),
            jax.ShapeDtypeStruct((n, ATTN_WIDTH), BF16),
            jax.ShapeDtypeStruct((n, SGU_WIDTH), BF16),
        ),
        grid=(n // GM_BM,),
        in_specs=[
            pl.BlockSpec(memory_space=pltpu.SMEM),
            pl.BlockSpec((GM_BM, D_MODEL), row_block),
            pl.BlockSpec((D_MODEL, 2 * D_MODEL), const, pipeline_mode=pl.Buffered(1)),
            pl.BlockSpec((1, 2 * D_MODEL), const),
            pl.BlockSpec((GM_BM, QKV_WIDTH), row_block),
            pl.BlockSpec((BLOCK, 2 * KV_WIDTH),
                         lambda i: (jnp.maximum(i * GM_SUB - 1, 0), ATTN_WIDTH // (2 * KV_WIDTH))),
            pl.BlockSpec((GM_BM, SGU_WIDTH), row_block),
            pl.BlockSpec((GM_BM, SGU_WIDTH), row_block),
            pl.BlockSpec((SGU_GROUPS, BLOCK, BLOCK), lambda i: (0, 0, 0)),
            pl.BlockSpec((BLOCK, SGU_WIDTH), const),
            pl.BlockSpec((1, SGU_WIDTH), const),
            pl.BlockSpec((1, SGU_WIDTH), const),
        ],
        out_specs=(
            pl.BlockSpec((GM_BM, 2 * D_MODEL), row_block),
            pl.BlockSpec((GM_BM, ATTN_WIDTH), row_block),
            pl.BlockSpec((GM_BM, SGU_WIDTH), row_block),
        ),
        scratch_shapes=[
            pltpu.VMEM((GM_BM + BLOCK, 2 * KV_WIDTH), BF16),
            pltpu.VMEM((SGU_GROUPS // 2, BLOCK, 2 * BLOCK), BF16),
            pltpu.VMEM((2, N_Q_HEADS, BLOCK, 2 * BLOCK), F32),
        ],
        compiler_params=pltpu.CompilerParams(
            dimension_semantics=("arbitrary",),
            vmem_limit_bytes=VMEM_LIMIT),
        name="gates_mixer",
    )(sinks, h2, w_gate, b_gate, qkv, qkv, u, z, ws, bs_full, ln_g, ln_b)


def _merge_kernel(attn_ref, sgu_ref, gt_ref, x_ref, pa_ref, pb_ref, wo_ref, o_ref, m_ref):
    attn = attn_ref[...]
    sgu = sgu_ref[...]
    for c in range(D_MODEL // MERGE_BN):
        cols = slice(c * MERGE_BN, (c + 1) * MERGE_BN)
        a = jnp.dot(attn, pa_ref[:, cols], preferred_element_type=F32)
        b = jnp.dot(sgu, pb_ref[:, cols], preferred_element_type=F32)
        g_a = gt_ref[:, cols].astype(F32)
        g_b = gt_ref[:, D_MODEL + c * MERGE_BN:D_MODEL + (c + 1) * MERGE_BN].astype(F32)
        m_ref[:, cols] = (g_a * a + g_b * b).astype(BF16)
    o_ref[...] = x_ref[...] + jnp.dot(m_ref[...], wo_ref[...], preferred_element_type=F32)


def _merge(attn, sgu, gates, x1, p_a, p_b, w_out):
    n = x1.shape[0]
    const = lambda i: (0, 0)
    return pl.pallas_call(
        _merge_kernel,
        out_shape=jax.ShapeDtypeStruct((n, D_MODEL), F32),
        grid=(n // MERGE_BM,),
        in_specs=[
            pl.BlockSpec((MERGE_BM, ATTN_WIDTH), lambda i: (i, 0)),
            pl.BlockSpec((MERGE_BM, SGU_WIDTH), lambda i: (i, 0)),
            pl.BlockSpec((MERGE_BM, 2 * D_MODEL), lambda i: (i, 0)),
            pl.BlockSpec((MERGE_BM, D_MODEL), lambda i: (i, 0)),
            pl.BlockSpec((ATTN_WIDTH, D_MODEL), const, pipeline_mode=pl.Buffered(1)),
            pl.BlockSpec((SGU_WIDTH, D_MODEL), const, pipeline_mode=pl.Buffered(1)),
            pl.BlockSpec((D_MODEL, D_MODEL), const, pipeline_mode=pl.Buffered(1)),
        ],
        out_specs=pl.BlockSpec((MERGE_BM, D_MODEL), lambda i: (i, 0)),
        scratch_shapes=[pltpu.VMEM((MERGE_BM, D_MODEL), BF16)],
        compiler_params=pltpu.CompilerParams(
            dimension_semantics=("arbitrary",),
            vmem_limit_bytes=VMEM_LIMIT),
        name="merge",
    )(attn, sgu, gates, x1, p_a, p_b, w_out)


def kernel(x, ffn1_norm, ffn1_w_gate, ffn1_w_up, ffn1_w_down, mix_norm, w_in, attn_sinks,
           sgu_norm_g, sgu_norm_b, sgu_w_s, sgu_b_s, w_proj_attn, w_proj_sgu,
           w_branch_gate, b_branch_gate, w_out, ffn2_norm, ffn2_w_gate, ffn2_w_up,
           ffn2_w_down, final_norm):
    batch, seq, d = x.shape
    assert (seq, d) == (SEQ, D_MODEL) and ffn1_norm.shape[0] == 1
    n = batch * seq
    xf = x.reshape(n, d)
    bf = lambda w: w.astype(BF16)
    row = lambda v: v.reshape(1, -1)

    later_weights = (w_in[0], w_branch_gate[0], w_proj_attn[0], w_proj_sgu[0], w_out[0],
                     ffn2_w_gate[0], ffn2_w_up[0], ffn2_w_down[0])
    (x1, h2, w_in_bf, w_bgate_bf, p_a_bf, p_b_bf, w_out_bf, f2_gate_bf, f2_up_bf, f2_down_bf) = _ffn(
        xf, row(ffn1_norm[0]), bf(ffn1_w_gate[0]), bf(ffn1_w_up[0]), bf(ffn1_w_down[0]),
        row(mix_norm[0]), final_norm=False, casts=later_weights)

    qkv, u, z = _proj(h2, w_in_bf)

    bs_full = jnp.repeat(sgu_b_s[0].T, SGU_GROUP_DIM, axis=1)
    gates, attn, sgu = _gates_mixer(h2, w_bgate_bf, row(b_branch_gate[0]), qkv, u, z,
                                    attn_sinks[0], sgu_w_s[0], bs_full,
                                    row(sgu_norm_g[0]), row(sgu_norm_b[0]))

    x2 = _merge(attn, sgu, gates, x1, p_a_bf, p_b_bf, w_out_bf)

    (out,) = _ffn(x2, row(ffn2_norm[0]), f2_gate_bf, f2_up_bf, f2_down_bf,
                  row(final_norm), final_norm=True)
    return out.reshape(batch, seq, d)
```

```python
import functools
import math

import jax
import jax.numpy as jnp
from jax import lax
from jax.experimental import pallas as pl
from jax.experimental.pallas import tpu as pltpu

D_MODEL = 2048
SEQ = 2048
HEAD_DIM = 64
N_Q_HEADS = 16
N_KV_HEADS = 4
GQA_GROUP = N_Q_HEADS // N_KV_HEADS
BLOCK = 128
ATTN_WIDTH = N_Q_HEADS * HEAD_DIM
KV_WIDTH = N_KV_HEADS * HEAD_DIM
QKV_WIDTH = ATTN_WIDTH + 2 * KV_WIDTH
SGU_GROUPS = 16
SGU_WIDTH = 1024
SGU_GROUP_DIM = SGU_WIDTH // SGU_GROUPS
D_FF = 5632
EPS = 1e-6
NEG_INF = -1e30

BF16 = jnp.bfloat16
F32 = jnp.float32

FFN_BM = 1024
FFN1_BF = 512
FFN2_BF = 768
PROJ_BM = 1024
PROJ_BN = 512
GM_BM = 256
MERGE_BM = 512
MERGE_BN = 512
NORM_ROWS = 128
VMEM_LIMIT = 56 * 1024 * 1024
FFN_VMEM_LIMIT = 60 * 1024 * 1024

ALIBI_SLOPES = tuple(2.0 ** (-8.0 * (i + 1) / N_Q_HEADS) for i in range(N_Q_HEADS))


def _rms_scale(y, g):
    ms = jnp.mean(y * y, axis=-1, keepdims=True)
    return y * lax.rsqrt(ms + EPS) * g


BF16_TILE_ROWS = 16


def _cast_slab_rows(n_rows, total_chunks):
    rows = BF16_TILE_ROWS
    while n_rows // rows > total_chunks:
        rows *= 2
    assert n_rows % rows == 0
    return rows


def _ffn_kernel(*refs, final_norm, n_casts, bf):
    n_out = 1 if final_norm else 2
    x_hbm, g_ref, wg_hbm, wu_hbm, wd_hbm, ng_ref = refs[:6]
    cast_src = refs[6:6 + n_casts]
    outs = refs[6 + n_casts:6 + n_casts + n_out + n_casts]
    o_ref = outs[0]
    hn_ref = None if final_norm else outs[1]
    cast_dst = outs[n_out:]
    scratch = refs[6 + n_casts + n_out + n_casts:]
    x_buf, h_ref, wg_buf, wu_buf, wd_buf, x_sem, w_sem = scratch[:7]
    cast_in = scratch[7:7 + n_casts]
    cast_out = scratch[7 + n_casts:7 + 2 * n_casts]
    if n_casts:
        cin_sem, cout_sem = scratch[7 + 2 * n_casts:]
    i = pl.program_id(0)
    n_blocks = pl.num_programs(0)
    n_full, tail_width = divmod(D_FF, bf)
    n_chunks = n_full + (1 if tail_width else 0)

    def cast_in_copy(k, slab, slot):
        rows = cast_in[k].shape[1]
        span = pl.ds(pl.multiple_of(slab * rows, rows), rows)
        return pltpu.make_async_copy(cast_src[k].at[span, :], cast_in[k].at[slot], cin_sem.at[k, slot])

    def cast_out_copy(k, slab, slot):
        rows = cast_out[k].shape[1]
        span = pl.ds(pl.multiple_of(slab * rows, rows), rows)
        return pltpu.make_async_copy(cast_out[k].at[slot], cast_dst[k].at[span, :], cout_sem.at[k, slot])

    total_chunks = n_blocks * n_chunks

    def cast_slab(k, t):
        n_slabs = cast_src[k].shape[0] // cast_in[k].shape[1]
        assert n_slabs >= 2
        redo = n_slabs - 1 - lax.rem(t - (n_slabs - 1), 2)
        return jnp.where(t < n_slabs, t, redo)

    def when(cond, guarded):
        return pl.when(cond) if guarded else (lambda f: f())

    def cast_prime():
        for k in range(n_casts):
            cast_in_copy(k, 0, 0).start()
            cast_in_copy(k, 1, 1).start()
        for k in range(n_casts):
            cast_in_copy(k, 0, 0).wait()

    def cast_dma_tail(t, guarded):
        slot = lax.rem(t, 2)
        for k in range(n_casts):
            cast_out_copy(k, cast_slab(k, t), slot).start()

        @when(t >= 1, guarded)
        def _():
            for k in range(n_casts):
                cast_out_copy(k, cast_slab(k, t - 1), 1 - slot).wait()

        @when(t + 2 < total_chunks, guarded)
        def _():
            for k in range(n_casts):
                cast_in_copy(k, cast_slab(k, t + 2), slot).start()

        @when(t + 1 < total_chunks, guarded)
        def _():
            for k in range(n_casts):
                cast_in_copy(k, cast_slab(k, t + 1), 1 - slot).wait()

    def cast_drain():
        last = total_chunks - 1
        for k in range(n_casts):
            cast_out_copy(k, cast_slab(k, last), lax.rem(last, 2)).wait()

    def x_copy(block):
        rows = pl.ds(pl.multiple_of(block * FFN_BM, FFN_BM), FFN_BM)
        return pltpu.make_async_copy(x_hbm.at[rows, :], x_buf, x_sem.at[0])

    def w_copies(chunk, slot, width=bf):
        start = chunk * bf
        span = pl.ds(start if isinstance(start, int) else pl.multiple_of(start, bf), width)
        part = pl.ds(0, width)
        return (
            pltpu.make_async_copy(wg_hbm.at[:, span], wg_buf.at[slot, :, part], w_sem.at[0, slot]),
            pltpu.make_async_copy(wu_hbm.at[:, span], wu_buf.at[slot, :, part], w_sem.at[1, slot]),
            pltpu.make_async_copy(wd_hbm.at[span, :], wd_buf.at[slot, part, :], w_sem.at[2, slot]),
        )

    first_slot = lax.rem(i * n_chunks, 2)

    @pl.when(i == 0)
    def _():
        x_copy(0).start()
        for cp in w_copies(0, 0):
            cp.start()
        cast_prime()

    x_copy(i).wait()

    def norm_body(r, carry):
        rows = pl.ds(pl.multiple_of(r * NORM_ROWS, NORM_ROWS), NORM_ROWS)
        xs = x_buf[rows, :]
        h_ref[rows, :] = _rms_scale(xs, g_ref[...]).astype(BF16)
        o_ref[rows, :] = xs
        return carry
    lax.fori_loop(0, FFN_BM // NORM_ROWS, norm_body, 0, unroll=True)

    @pl.when(i + 1 < n_blocks)
    def _():
        x_copy(i + 1).start()

    def chunk_compute(c, slot, width, guarded):
        t = i * n_chunks + c
        cast_slot = lax.rem(t, 2)
        for k in range(n_casts):
            cast_out[k][cast_slot] = cast_in[k][cast_slot].astype(BF16)

        h = h_ref[...]
        gate = jnp.dot(h, wg_buf[slot, :, :width], preferred_element_type=F32)
        up = jnp.dot(h, wu_buf[slot, :, :width], preferred_element_type=F32)
        act = (gate * jax.nn.sigmoid(gate) * up * 0.5).astype(BF16)
        o_ref[...] += jnp.dot(act, wd_buf[slot, :width, :], preferred_element_type=F32)

        if n_casts:
            cast_dma_tail(t, guarded)

    def chunk_body(c, carry, *, guarded):
        slot = lax.rem(first_slot + c, 2)
        for cp in w_copies(c, slot):
            cp.wait()

        if tail_width:
            @pl.when(c + 1 < n_full)
            def _():
                for cp in w_copies(c + 1, 1 - slot):
                    cp.start()

            @pl.when(c + 1 == n_full)
            def _():
                for cp in w_copies(n_full, 1 - slot, tail_width):
                    cp.start()
        else:
            @when(jnp.logical_or(c + 1 < n_full, i + 1 < n_blocks), guarded)
            def _():
                for cp in w_copies(lax.rem(c + 1, n_full), 1 - slot):
                    cp.start()

        chunk_compute(c, slot, bf, guarded)
        return carry

    def tail_chunk(guarded):
        slot = lax.rem(first_slot + n_full, 2)
        for cp in w_copies(n_full, slot, tail_width):
            cp.wait()

        @when(i + 1 < n_blocks, guarded)
        def _():
            for cp in w_copies(0, 1 - slot):
                cp.start()

        chunk_compute(n_full, slot, tail_width, guarded)

    def all_chunks(guarded):
        lax.fori_loop(0, n_full, functools.partial(chunk_body, guarded=guarded), 0)
        if tail_width:
            tail_chunk(guarded)

    is_edge_block = jnp.logical_or(i == 0, i == n_blocks - 1)

    @pl.when(is_edge_block)
    def _():
        all_chunks(guarded=True)

    @pl.when(jnp.logical_not(is_edge_block))
    def _():
        all_chunks(guarded=False)

    def final_body(r, carry):
        rows = pl.ds(pl.multiple_of(r * NORM_ROWS, NORM_ROWS), NORM_ROWS)
        normed = _rms_scale(o_ref[rows, :], ng_ref[...])
        if final_norm:
            o_ref[rows, :] = normed
        else:
            hn_ref[rows, :] = normed.astype(BF16)
        return carry
    lax.fori_loop(0, FFN_BM // NORM_ROWS, final_body, 0, unroll=True)

    if n_casts:
        @pl.when(i == n_blocks - 1)
        def _():
            cast_drain()


def _ffn(x, norm_g, w_gate, w_up, w_down, next_g, *, final_norm, bf, casts=()):
    n = x.shape[0]
    n_blocks = n // FFN_BM
    n_chunks = pl.cdiv(D_FF, bf)
    row_block = lambda i: (i, 0)
    const = lambda i: (0, 0)
    hbm = pl.BlockSpec(memory_space=pl.ANY)
    out_shape = [jax.ShapeDtypeStruct((n, D_MODEL), F32)]
    out_specs = [pl.BlockSpec((FFN_BM, D_MODEL), row_block)]
    if not final_norm:
        out_shape.append(jax.ShapeDtypeStruct((n, D_MODEL), BF16))
        out_specs.append(pl.BlockSpec((FFN_BM, D_MODEL), row_block))
    scratch = [
        pltpu.VMEM((FFN_BM, D_MODEL), F32),
        pltpu.VMEM((FFN_BM, D_MODEL), BF16),
        pltpu.VMEM((2, D_MODEL, bf), BF16),
        pltpu.VMEM((2, D_MODEL, bf), BF16),
        pltpu.VMEM((2, bf, D_MODEL), BF16),
        pltpu.SemaphoreType.DMA((1,)),
        pltpu.SemaphoreType.DMA((3, 2)),
    ]
    if casts:
        slab_rows = [_cast_slab_rows(w.shape[0], n_blocks * n_chunks) for w in casts]
        out_shape += [jax.ShapeDtypeStruct(w.shape, BF16) for w in casts]
        out_specs += [hbm] * len(casts)
        scratch += [pltpu.VMEM((2, r, w.shape[1]), F32) for r, w in zip(slab_rows, casts)]
        scratch += [pltpu.VMEM((2, r, w.shape[1]), BF16) for r, w in zip(slab_rows, casts)]
        scratch += [pltpu.SemaphoreType.DMA((len(casts), 2))] * 2
    return pl.pallas_call(
        functools.partial(_ffn_kernel, final_norm=final_norm, n_casts=len(casts), bf=bf),
        out_shape=tuple(out_shape),
        grid=(n_blocks,),
        in_specs=[hbm, pl.BlockSpec((1, D_MODEL), const), hbm, hbm, hbm,
                  pl.BlockSpec((1, D_MODEL), const)] + [hbm] * len(casts),
        out_specs=tuple(out_specs),
        scratch_shapes=scratch,
        compiler_params=pltpu.CompilerParams(
            dimension_semantics=("arbitrary",),
            vmem_limit_bytes=FFN_VMEM_LIMIT),
        name="ffn_final" if final_norm else "ffn",
    )(x, norm_g, w_gate, w_up, w_down, next_g, *casts)


def _gelu_tanh(x):
    c = math.sqrt(2.0 / math.pi)
    return 0.5 * x * (1.0 + jnp.tanh(c * (x + 0.044715 * (x * x * x))))


def _proj_kernel(h_ref, w_ref, qkv_ref, u_ref, z_ref):
    h = h_ref[...]
    for c in range(2 * SGU_WIDTH // PROJ_BN):
        w_cols = slice(QKV_WIDTH + c * PROJ_BN, QKV_WIDTH + (c + 1) * PROJ_BN)
        acc = jnp.dot(h, w_ref[:, w_cols], preferred_element_type=F32)
        dst = u_ref if c * PROJ_BN < SGU_WIDTH else z_ref
        off = (c * PROJ_BN) % SGU_WIDTH
        dst[:, off:off + PROJ_BN] = _gelu_tanh(acc).astype(BF16)
    for c in range(QKV_WIDTH // PROJ_BN):
        cols = slice(c * PROJ_BN, (c + 1) * PROJ_BN)
        acc = jnp.dot(h, w_ref[:, cols], preferred_element_type=F32)
        if (c + 1) * PROJ_BN <= ATTN_WIDTH:
            acc = acc * (HEAD_DIM ** -0.5)
        qkv_ref[:, cols] = acc.astype(BF16)


def _proj(h2, w_in):
    n = h2.shape[0]
    row_block = lambda i: (i, 0)
    const = lambda i: (0, 0)
    return pl.pallas_call(
        _proj_kernel,
        out_shape=(
            jax.ShapeDtypeStruct((n, QKV_WIDTH), BF16),
            jax.ShapeDtypeStruct((n, SGU_WIDTH), BF16),
            jax.ShapeDtypeStruct((n, SGU_WIDTH), BF16),
        ),
        grid=(n // PROJ_BM,),
        in_specs=[
            pl.BlockSpec((PROJ_BM, D_MODEL), row_block),
            pl.BlockSpec((D_MODEL, QKV_WIDTH + 2 * SGU_WIDTH), const, pipeline_mode=pl.Buffered(1)),
        ],
        out_specs=(
            pl.BlockSpec((PROJ_BM, QKV_WIDTH), row_block),
            pl.BlockSpec((PROJ_BM, SGU_WIDTH), row_block),
            pl.BlockSpec((PROJ_BM, SGU_WIDTH), row_block),
        ),
        compiler_params=pltpu.CompilerParams(
            dimension_semantics=("arbitrary",),
            vmem_limit_bytes=VMEM_LIMIT),
        name="proj",
    )(h2, w_in)


GM_SUB = GM_BM // BLOCK
GM_GATE_CHUNKS = 2 * D_MODEL // PROJ_BN
K_OFF = ATTN_WIDTH


def _mix_pieces(r, no_prev, sinks_ref, qkv_ref, u_ref, z_ref, bs_ref, lng_ref, lnb_ref,
                attn_ref, sgu_ref, kv_buf, ws_buf, bias_buf):
    rows = slice(r * BLOCK, (r + 1) * BLOCK)
    band = slice(r * BLOCK, (r + 2) * BLOCK)
    group_width = GQA_GROUP * HEAD_DIM

    def attention(hk):
        k_band = kv_buf[band, hk * HEAD_DIM:(hk + 1) * HEAD_DIM]
        v_band = kv_buf[band, KV_WIDTH + hk * HEAD_DIM:KV_WIDTH + (hk + 1) * HEAD_DIM]
        heads = range(hk * GQA_GROUP, (hk + 1) * GQA_GROUP)
        q = jnp.concatenate([qkv_ref[rows, h * HEAD_DIM:(h + 1) * HEAD_DIM] for h in heads], axis=0)
        s_all = lax.dot_general(q, k_band, (((1,), (1,)), ((), ())), preferred_element_type=F32)
        probs, denoms = [], []
        for gq, h in enumerate(heads):
            s = s_all[gq * BLOCK:(gq + 1) * BLOCK] + bias_buf[no_prev, h]
            sink = sinks_ref[h]
            m = jnp.maximum(jnp.max(s, axis=-1, keepdims=True), sink)
            p = jnp.exp(s - m)
            denoms.append(jnp.sum(p, axis=-1, keepdims=True) + jnp.exp(sink - m))
            probs.append(p.astype(BF16))
        o_all = jnp.dot(jnp.concatenate(probs, axis=0), v_band, preferred_element_type=F32)
        outs = [o_all[gq * BLOCK:(gq + 1) * BLOCK] / denoms[gq] for gq in range(GQA_GROUP)]
        attn_ref[rows, hk * group_width:(hk + 1) * group_width] = (
            jnp.concatenate(outs, axis=-1).astype(BF16))

    def gating():
        zc = z_ref[rows, :].astype(F32)
        mu = jnp.mean(zc, axis=-1, keepdims=True)
        zc = zc - mu
        var = jnp.mean(zc * zc, axis=-1, keepdims=True)
        zl = (zc * lax.rsqrt(var + EPS) * lng_ref[...] + lnb_ref[...]).astype(BF16)
        first_group = lax.broadcasted_iota(jnp.int32, (BLOCK, 2 * SGU_GROUP_DIM), 1) < SGU_GROUP_DIM
        zero = jnp.zeros((BLOCK, 2 * SGU_GROUP_DIM), BF16)
        mixed = []
        for j in range(SGU_GROUPS // 2):
            z_pair = zl[:, j * 2 * SGU_GROUP_DIM:(j + 1) * 2 * SGU_GROUP_DIM]
            z_diag = jnp.concatenate([jnp.where(first_group, z_pair, zero),
                                      jnp.where(first_group, zero, z_pair)], axis=0)
            mixed.append(jnp.dot(ws_buf[j], z_diag, preferred_element_type=F32))
        mixed = jnp.concatenate(mixed, axis=-1) + bs_ref[...]
        sgu_ref[rows, :] = (u_ref[rows, :].astype(F32) * mixed).astype(BF16)

    return [functools.partial(attention, hk) for hk in range(N_KV_HEADS)] + [gating]


def _gates_mixer_kernel(sinks_ref, h_ref, wg_ref, bg_ref, qkv_ref, halo_ref, u_ref, z_ref, ws_ref,
                        bs_ref, lng_ref, lnb_ref, gt_ref, attn_ref, sgu_ref,
                        kv_buf, ws_buf, bias_buf):
    i = pl.program_id(0)

    @pl.when(i == 0)
    def _():
        tq = lax.broadcasted_iota(jnp.int32, (BLOCK, BLOCK), 0)
        ts = lax.broadcasted_iota(jnp.int32, (BLOCK, BLOCK), 1)
        causal = ts <= tq
        for g in range(SGU_GROUPS):
            ws_buf[g // 2, :, (g % 2) * BLOCK:(g % 2 + 1) * BLOCK] = (
                jnp.where(causal, ws_ref[g], 0.0).astype(BF16))

        qpos = lax.broadcasted_iota(jnp.int32, (BLOCK, 2 * BLOCK), 0) + BLOCK
        kpos = lax.broadcasted_iota(jnp.int32, (BLOCK, 2 * BLOCK), 1)
        dist_i = qpos - kpos
        in_window = (dist_i >= 0) & (dist_i < BLOCK)
        dist = jnp.abs(dist_i).astype(F32)
        for h in range(N_Q_HEADS):
            alibi = -ALIBI_SLOPES[h] * dist
            bias_buf[0, h] = jnp.where(in_window, alibi, NEG_INF)
            bias_buf[1, h] = jnp.where(in_window & (kpos >= BLOCK), alibi, NEG_INF)

    kv_buf[0:BLOCK, :] = halo_ref[...]
    kv_buf[BLOCK:, :] = qkv_ref[:, K_OFF:]

    step_starts_sequence = ((i % (SEQ // GM_BM)) == 0).astype(jnp.int32)

    pieces = []
    for r in range(GM_SUB):
        pieces += _mix_pieces(r, step_starts_sequence if r == 0 else 0,
                              sinks_ref, qkv_ref, u_ref, z_ref, bs_ref, lng_ref, lnb_ref,
                              attn_ref, sgu_ref, kv_buf, ws_buf, bias_buf)

    h = h_ref[...]
    for c in range(GM_GATE_CHUNKS):
        for piece in pieces[c * len(pieces) // GM_GATE_CHUNKS:(c + 1) * len(pieces) // GM_GATE_CHUNKS]:
            piece()
        cols = slice(c * PROJ_BN, (c + 1) * PROJ_BN)
        acc = jnp.dot(h, wg_ref[:, cols], preferred_element_type=F32)
        gt_ref[:, cols] = jax.nn.sigmoid(acc + bg_ref[:, cols]).astype(BF16)


def _gates_mixer(h2, w_gate, b_gate, qkv, u, z, sinks, ws, bs_full, ln_g, ln_b):
    n = h2.shape[0]
    row_block = lambda i: (i, 0)
    const = lambda i: (0, 0)
    return pl.pallas_call(
        _gates_mixer_kernel,
        out_shape=(
            jax.ShapeDtypeStruct((n, 2 * D_MODEL), BF16),
            jax.ShapeDtypeStruct((n, ATTN_WIDTH), BF16),
            jax.ShapeDtypeStruct((n, SGU_WIDTH), BF16),
        ),
        grid=(n // GM_BM,),
        in_specs=[
            pl.BlockSpec(memory_space=pltpu.SMEM),
            pl.BlockSpec((GM_BM, D_MODEL), row_block),
            pl.BlockSpec((D_MODEL, 2 * D_MODEL), const, pipeline_mode=pl.Buffered(1)),
            pl.BlockSpec((1, 2 * D_MODEL), const),
            pl.BlockSpec((GM_BM, QKV_WIDTH), row_block),
            pl.BlockSpec((BLOCK, 2 * KV_WIDTH),
                         lambda i: (jnp.maximum(i * GM_SUB - 1, 0), ATTN_WIDTH // (2 * KV_WIDTH))),
            pl.BlockSpec((GM_BM, SGU_WIDTH), row_block),
            pl.BlockSpec((GM_BM, SGU_WIDTH), row_block),
            pl.BlockSpec((SGU_GROUPS, BLOCK, BLOCK), lambda i: (0, 0, 0)),
            pl.BlockSpec((BLOCK, SGU_WIDTH), const),
            pl.BlockSpec((1, SGU_WIDTH), const),
            pl.BlockSpec((1, SGU_WIDTH), const),
        ],
        out_specs=(
            pl.BlockSpec((GM_BM, 2 * D_MODEL), row_block),
            pl.BlockSpec((GM_BM, ATTN_WIDTH), row_block),
            pl.BlockSpec((GM_BM, SGU_WIDTH), row_block),
        ),
        scratch_shapes=[
            pltpu.VMEM((GM_BM + BLOCK, 2 * KV_WIDTH), BF16),
            pltpu.VMEM((SGU_GROUPS // 2, BLOCK, 2 * BLOCK), BF16),
            pltpu.VMEM((2, N_Q_HEADS, BLOCK, 2 * BLOCK), F32),
        ],
        compiler_params=pltpu.CompilerParams(
            dimension_semantics=("arbitrary",),
            vmem_limit_bytes=VMEM_LIMIT),
        name="gates_mixer",
    )(sinks, h2, w_gate, b_gate, qkv, qkv, u, z, ws, bs_full, ln_g, ln_b)


def _merge_kernel(attn_ref, sgu_ref, gt_ref, x_ref, pa_ref, pb_ref, wo_ref, o_ref, m_ref):
    attn = attn_ref[...]
    sgu = sgu_ref[...]
    for c in range(D_MODEL // MERGE_BN):
        cols = slice(c * MERGE_BN, (c + 1) * MERGE_BN)
        a = jnp.dot(attn, pa_ref[:, cols], preferred_element_type=F32)
        b = jnp.dot(sgu, pb_ref[:, cols], preferred_element_type=F32)
        g_a = gt_ref[:, cols].astype(F32)
        g_b = gt_ref[:, D_MODEL + c * MERGE_BN:D_MODEL + (c + 1) * MERGE_BN].astype(F32)
        m_ref[:, cols] = (g_a * a + g_b * b).astype(BF16)
    o_ref[...] = x_ref[...] + jnp.dot(m_ref[...], wo_ref[...], preferred_element_type=F32)


def _merge(attn, sgu, gates, x1, p_a, p_b, w_out):
    n = x1.shape[0]
    const = lambda i: (0, 0)
    return pl.pallas_call(
        _merge_kernel,
        out_shape=jax.ShapeDtypeStruct((n, D_MODEL), F32),
        grid=(n // MERGE_BM,),
        in_specs=[
            pl.BlockSpec((MERGE_BM, ATTN_WIDTH), lambda i: (i, 0)),
            pl.BlockSpec((MERGE_BM, SGU_WIDTH), lambda i: (i, 0)),
            pl.BlockSpec((MERGE_BM, 2 * D_MODEL), lambda i: (i, 0)),
            pl.BlockSpec((MERGE_BM, D_MODEL), lambda i: (i, 0)),
            pl.BlockSpec((ATTN_WIDTH, D_MODEL), const, pipeline_mode=pl.Buffered(1)),
            pl.BlockSpec((SGU_WIDTH, D_MODEL), const, pipeline_mode=pl.Buffered(1)),
            pl.BlockSpec((D_MODEL, D_MODEL), const, pipeline_mode=pl.Buffered(1)),
        ],
        out_specs=pl.BlockSpec((MERGE_BM, D_MODEL), lambda i: (i, 0)),
        scratch_shapes=[pltpu.VMEM((MERGE_BM, D_MODEL), BF16)],
        compiler_params=pltpu.CompilerParams(
            dimension_semantics=("arbitrary",),
            vmem_limit_bytes=VMEM_LIMIT),
        name="merge",
    )(attn, sgu, gates, x1, p_a, p_b, w_out)


def kernel(x, ffn1_norm, ffn1_w_gate, ffn1_w_up, ffn1_w_down, mix_norm, w_in, attn_sinks,
           sgu_norm_g, sgu_norm_b, sgu_w_s, sgu_b_s, w_proj_attn, w_proj_sgu,
           w_branch_gate, b_branch_gate, w_out, ffn2_norm, ffn2_w_gate, ffn2_w_up,
           ffn2_w_down, final_norm):
    batch, seq, d = x.shape
    assert (seq, d) == (SEQ, D_MODEL) and ffn1_norm.shape[0] == 1
    n = batch * seq
    xf = x.reshape(n, d)
    bf = lambda w: w.astype(BF16)
    row = lambda v: v.reshape(1, -1)

    later_weights = (w_in[0], w_branch_gate[0], w_proj_attn[0], w_proj_sgu[0], w_out[0],
                     ffn2_w_gate[0], ffn2_w_up[0], ffn2_w_down[0])
    (x1, h2, w_in_bf, w_bgate_bf, p_a_bf, p_b_bf, w_out_bf, f2_gate_bf, f2_up_bf, f2_down_bf) = _ffn(
        xf, row(ffn1_norm[0]), bf(ffn1_w_gate[0]), bf(ffn1_w_up[0]), bf(ffn1_w_down[0]),
        row(mix_norm[0]), final_norm=False, bf=FFN1_BF, casts=later_weights)

    qkv, u, z = _proj(h2, w_in_bf)

    bs_full = jnp.repeat(sgu_b_s[0].T, SGU_GROUP_DIM, axis=1)
    gates, attn, sgu = _gates_mixer(h2, w_bgate_bf, row(b_branch_gate[0]), qkv, u, z,
                                    attn_sinks[0], sgu_w_s[0], bs_full,
                                    row(sgu_norm_g[0]), row(sgu_norm_b[0]))

    x2 = _merge(attn, sgu, gates, x1, p_a_bf, p_b_bf, w_out_bf)

    (out,) = _ffn(x2, row(ffn2_norm[0]), f2_gate_bf, f2_up_bf, f2_down_bf,
                  row(final_norm), final_norm=True, bf=FFN2_BF)
    return out.reshape(batch, seq, d)
```

```python
import functools
import math

import jax
import jax.numpy as jnp
from jax import lax
from jax.experimental import pallas as pl
from jax.experimental.pallas import tpu as pltpu

D_MODEL = 2048
SEQ = 2048
HEAD_DIM = 64
N_Q_HEADS = 16
N_KV_HEADS = 4
GQA_GROUP = N_Q_HEADS // N_KV_HEADS
BLOCK = 128
ATTN_WIDTH = N_Q_HEADS * HEAD_DIM
KV_WIDTH = N_KV_HEADS * HEAD_DIM
QKV_WIDTH = ATTN_WIDTH + 2 * KV_WIDTH
SGU_GROUPS = 16
SGU_WIDTH = 1024
SGU_GROUP_DIM = SGU_WIDTH // SGU_GROUPS
D_FF = 5632
EPS = 1e-6
NEG_INF = -1e30

BF16 = jnp.bfloat16
F32 = jnp.float32

FFN_BM = 1024
FFN1_BF = 768
FFN2_BF = 1024
PROJ_BM = 1024
PROJ_BN = 512
GM_BM = 512
GM_PART = 256
MERGE_BM = 512
MERGE_BN = 256
NORM_ROWS = 128
MXU_COLS = 256
VMEM_LIMIT = 56 * 1024 * 1024
FFN_VMEM_LIMIT = 60 * 1024 * 1024

ALIBI_SLOPES = tuple(2.0 ** (-8.0 * (i + 1) / N_Q_HEADS) for i in range(N_Q_HEADS))


def _rms_scale(y, g):
    ms = jnp.mean(y * y, axis=-1, keepdims=True)
    return y * lax.rsqrt(ms + EPS) * g


BF16_TILE_ROWS = 16


def _cast_slab_rows(n_rows, total_chunks):
    rows = BF16_TILE_ROWS
    while n_rows // rows > total_chunks:
        rows *= 2
    assert n_rows % rows == 0
    return rows


def _ffn_kernel(*refs, final_norm, n_casts, bf):
    n_out = 1 if final_norm else 2
    x_hbm, g_ref, wg_hbm, wu_hbm, wd_hbm, ng_ref = refs[:6]
    cast_src = refs[6:6 + n_casts]
    outs = refs[6 + n_casts:6 + n_casts + n_out + n_casts]
    o_hbm = outs[0]
    hn_ref = None if final_norm else outs[1]
    cast_dst = outs[n_out:]
    scratch = refs[6 + n_casts + n_out + n_casts:]
    o_buf, h_ref, wg_buf, wu_buf, wd_buf, x_sem, o_sem, w_sem = scratch[:8]
    cast_in = scratch[8:8 + n_casts]
    cast_out = scratch[8 + n_casts:8 + 2 * n_casts]
    if n_casts:
        cin_sem, cout_sem = scratch[8 + 2 * n_casts:]
    i = pl.program_id(0)
    n_blocks = pl.num_programs(0)
    n_full, tail_width = divmod(D_FF, bf)
    n_chunks = n_full + (1 if tail_width else 0)

    o_slot = lax.rem(i, 2)
    o_ref = o_buf.at[o_slot]

    def cast_in_copy(k, slab, slot):
        rows = cast_in[k].shape[1]
        span = pl.ds(pl.multiple_of(slab * rows, rows), rows)
        return pltpu.make_async_copy(cast_src[k].at[span, :], cast_in[k].at[slot], cin_sem.at[k, slot])

    def cast_out_copy(k, slab, slot):
        rows = cast_out[k].shape[1]
        span = pl.ds(pl.multiple_of(slab * rows, rows), rows)
        return pltpu.make_async_copy(cast_out[k].at[slot], cast_dst[k].at[span, :], cout_sem.at[k, slot])

    total_chunks = n_blocks * n_chunks

    def cast_slab(k, t):
        n_slabs = cast_src[k].shape[0] // cast_in[k].shape[1]
        assert n_slabs >= 2
        redo = n_slabs - 1 - lax.rem(t - (n_slabs - 1), 2)
        return jnp.where(t < n_slabs, t, redo)

    def when(cond, guarded):
        return pl.when(cond) if guarded else (lambda f: f())

    def cast_prime():
        for k in range(n_casts):
            cast_in_copy(k, 0, 0).start()
            cast_in_copy(k, 1, 1).start()
        for k in range(n_casts):
            cast_in_copy(k, 0, 0).wait()

    def cast_dma_tail(t, guarded):
        slot = lax.rem(t, 2)
        for k in range(n_casts):
            cast_out_copy(k, cast_slab(k, t), slot).start()

        @when(t >= 1, guarded)
        def _():
            for k in range(n_casts):
                cast_out_copy(k, cast_slab(k, t - 1), 1 - slot).wait()

        @when(t + 2 < total_chunks, guarded)
        def _():
            for k in range(n_casts):
                cast_in_copy(k, cast_slab(k, t + 2), slot).start()

        @when(t + 1 < total_chunks, guarded)
        def _():
            for k in range(n_casts):
                cast_in_copy(k, cast_slab(k, t + 1), 1 - slot).wait()

    def cast_drain():
        last = total_chunks - 1
        for k in range(n_casts):
            cast_out_copy(k, cast_slab(k, last), lax.rem(last, 2)).wait()

    def block_rows(block):
        return pl.ds(pl.multiple_of(block * FFN_BM, FFN_BM), FFN_BM)

    def x_copy(block, slot):
        return pltpu.make_async_copy(x_hbm.at[block_rows(block), :], o_buf.at[slot], x_sem.at[slot])

    def o_copy(block, slot):
        return pltpu.make_async_copy(o_buf.at[slot], o_hbm.at[block_rows(block), :], o_sem.at[slot])

    def w_copies(chunk, slot, width=bf):
        start = chunk * bf
        span = pl.ds(start if isinstance(start, int) else pl.multiple_of(start, bf), width)
        part = pl.ds(0, width)
        return (
            pltpu.make_async_copy(wg_hbm.at[:, span], wg_buf.at[slot, :, part], w_sem.at[0, slot]),
            pltpu.make_async_copy(wu_hbm.at[:, span], wu_buf.at[slot, :, part], w_sem.at[1, slot]),
            pltpu.make_async_copy(wd_hbm.at[span, :], wd_buf.at[slot, part, :], w_sem.at[2, slot]),
        )

    first_slot = lax.rem(i * n_chunks, 2)

    @pl.when(i == 0)
    def _():
        x_copy(0, 0).start()
        for cp in w_copies(0, 0):
            cp.start()
        cast_prime()

    x_copy(i, o_slot).wait()

    def norm_body(r, carry):
        rows = pl.ds(pl.multiple_of(r * NORM_ROWS, NORM_ROWS), NORM_ROWS)
        h_ref[rows, :] = _rms_scale(o_ref[rows, :], g_ref[...]).astype(BF16)
        return carry
    lax.fori_loop(0, FFN_BM // NORM_ROWS, norm_body, 0, unroll=True)

    def swap_other_buffer(guarded):
        @when(i >= 1, guarded)
        def _():
            o_copy(i - 1, 1 - o_slot).wait()

        @when(i + 1 < n_blocks, guarded)
        def _():
            x_copy(i + 1, 1 - o_slot).start()

    def chunk_compute(c, slot, width, guarded):
        t = i * n_chunks + c
        cast_slot = lax.rem(t, 2)
        for k in range(n_casts):
            cast_out[k][cast_slot] = cast_in[k][cast_slot].astype(BF16)

        h = h_ref[...]
        acts = []
        for j in range(width // MXU_COLS):
            cols = slice(j * MXU_COLS, (j + 1) * MXU_COLS)
            gate = jnp.dot(h, wg_buf[slot, :, cols], preferred_element_type=F32)
            up = jnp.dot(h, wu_buf[slot, :, cols], preferred_element_type=F32)
            acts.append((gate * jax.nn.sigmoid(gate) * up * 0.5).astype(BF16))
        act = jnp.concatenate(acts, axis=-1)
        o_ref[...] += jnp.dot(act, wd_buf[slot, :width, :], preferred_element_type=F32)

        if n_casts:
            cast_dma_tail(t, guarded)

    def chunk_body(c, carry, *, guarded):
        slot = lax.rem(first_slot + c, 2)
        for cp in w_copies(c, slot):
            cp.wait()

        @pl.when(c == 1)
        def _():
            swap_other_buffer(guarded)

        if tail_width:
            @pl.when(c + 1 < n_full)
            def _():
                for cp in w_copies(c + 1, 1 - slot):
                    cp.start()

            @pl.when(c + 1 == n_full)
            def _():
                for cp in w_copies(n_full, 1 - slot, tail_width):
                    cp.start()
        else:
            @when(jnp.logical_or(c + 1 < n_full, i + 1 < n_blocks), guarded)
            def _():
                for cp in w_copies(lax.rem(c + 1, n_full), 1 - slot):
                    cp.start()

        chunk_compute(c, slot, bf, guarded)
        return carry

    def tail_chunk(guarded):
        slot = lax.rem(first_slot + n_full, 2)
        for cp in w_copies(n_full, slot, tail_width):
            cp.wait()

        @when(i + 1 < n_blocks, guarded)
        def _():
            for cp in w_copies(0, 1 - slot):
                cp.start()

        chunk_compute(n_full, slot, tail_width, guarded)

    def all_chunks(guarded):
        lax.fori_loop(0, n_full, functools.partial(chunk_body, guarded=guarded), 0)
        if tail_width:
            tail_chunk(guarded)

    is_edge_block = jnp.logical_or(i == 0, i == n_blocks - 1)

    @pl.when(is_edge_block)
    def _():
        all_chunks(guarded=True)

    @pl.when(jnp.logical_not(is_edge_block))
    def _():
        all_chunks(guarded=False)

    def final_body(r, carry):
        rows = pl.ds(pl.multiple_of(r * NORM_ROWS, NORM_ROWS), NORM_ROWS)
        normed = _rms_scale(o_ref[rows, :], ng_ref[...])
        if final_norm:
            o_ref[rows, :] = normed
        else:
            hn_ref[rows, :] = normed.astype(BF16)
        return carry
    lax.fori_loop(0, FFN_BM // NORM_ROWS, final_body, 0, unroll=True)

    o_copy(i, o_slot).start()

    @pl.when(i == n_blocks - 1)
    def _():
        o_copy(i, o_slot).wait()
        if n_casts:
            cast_drain()


def _ffn(x, norm_g, w_gate, w_up, w_down, next_g, *, final_norm, bf, casts=()):
    n = x.shape[0]
    n_blocks = n // FFN_BM
    n_chunks = pl.cdiv(D_FF, bf)
    row_block = lambda i: (i, 0)
    const = lambda i: (0, 0)
    hbm = pl.BlockSpec(memory_space=pl.ANY)
    assert D_FF // bf >= 2
    out_shape = [jax.ShapeDtypeStruct((n, D_MODEL), F32)]
    out_specs = [hbm]
    if not final_norm:
        out_shape.append(jax.ShapeDtypeStruct((n, D_MODEL), BF16))
        out_specs.append(pl.BlockSpec((FFN_BM, D_MODEL), row_block))
    scratch = [
        pltpu.VMEM((2, FFN_BM, D_MODEL), F32),
        pltpu.VMEM((FFN_BM, D_MODEL), BF16),
        pltpu.VMEM((2, D_MODEL, bf), BF16),
        pltpu.VMEM((2, D_MODEL, bf), BF16),
        pltpu.VMEM((2, bf, D_MODEL), BF16),
        pltpu.SemaphoreType.DMA((2,)),
        pltpu.SemaphoreType.DMA((2,)),
        pltpu.SemaphoreType.DMA((3, 2)),
    ]
    if casts:
        slab_rows = [_cast_slab_rows(w.shape[0], n_blocks * n_chunks) for w in casts]
        out_shape += [jax.ShapeDtypeStruct(w.shape, BF16) for w in casts]
        out_specs += [hbm] * len(casts)
        scratch += [pltpu.VMEM((2, r, w.shape[1]), F32) for r, w in zip(slab_rows, casts)]
        scratch += [pltpu.VMEM((2, r, w.shape[1]), BF16) for r, w in zip(slab_rows, casts)]
        scratch += [pltpu.SemaphoreType.DMA((len(casts), 2))] * 2
    return pl.pallas_call(
        functools.partial(_ffn_kernel, final_norm=final_norm, n_casts=len(casts), bf=bf),
        out_shape=tuple(out_shape),
        grid=(n_blocks,),
        in_specs=[hbm, pl.BlockSpec((1, D_MODEL), const), hbm, hbm, hbm,
                  pl.BlockSpec((1, D_MODEL), const)] + [hbm] * len(casts),
        out_specs=tuple(out_specs),
        scratch_shapes=scratch,
        compiler_params=pltpu.CompilerParams(
            dimension_semantics=("arbitrary",),
            vmem_limit_bytes=FFN_VMEM_LIMIT),
        name="ffn_final" if final_norm else "ffn",
    )(x, norm_g, w_gate, w_up, w_down, next_g, *casts)


def _gelu_tanh(x):
    c = math.sqrt(2.0 / math.pi)
    return 0.5 * x * (1.0 + jnp.tanh(c * (x + 0.044715 * (x * x * x))))


def _proj_kernel(h_ref, w_ref, qkv_ref, u_ref, z_ref):
    h = h_ref[...]
    for c in range(2 * SGU_WIDTH // PROJ_BN):
        w_cols = slice(QKV_WIDTH + c * PROJ_BN, QKV_WIDTH + (c + 1) * PROJ_BN)
        acc = jnp.dot(h, w_ref[:, w_cols], preferred_element_type=F32)
        dst = u_ref if c * PROJ_BN < SGU_WIDTH else z_ref
        off = (c * PROJ_BN) % SGU_WIDTH
        dst[:, off:off + PROJ_BN] = _gelu_tanh(acc).astype(BF16)
    for c in range(QKV_WIDTH // PROJ_BN):
        cols = slice(c * PROJ_BN, (c + 1) * PROJ_BN)
        acc = jnp.dot(h, w_ref[:, cols], preferred_element_type=F32)
        if (c + 1) * PROJ_BN <= ATTN_WIDTH:
            acc = acc * (HEAD_DIM ** -0.5)
        qkv_ref[:, cols] = acc.astype(BF16)


def _proj(h2, w_in):
    n = h2.shape[0]
    row_block = lambda i: (i, 0)
    const = lambda i: (0, 0)
    return pl.pallas_call(
        _proj_kernel,
        out_shape=(
            jax.ShapeDtypeStruct((n, QKV_WIDTH), BF16),
            jax.ShapeDtypeStruct((n, SGU_WIDTH), BF16),
            jax.ShapeDtypeStruct((n, SGU_WIDTH), BF16),
        ),
        grid=(n // PROJ_BM,),
        in_specs=[
            pl.BlockSpec((PROJ_BM, D_MODEL), row_block),
            pl.BlockSpec((D_MODEL, QKV_WIDTH + 2 * SGU_WIDTH), const, pipeline_mode=pl.Buffered(1)),
        ],
        out_specs=(
            pl.BlockSpec((PROJ_BM, QKV_WIDTH), row_block),
            pl.BlockSpec((PROJ_BM, SGU_WIDTH), row_block),
            pl.BlockSpec((PROJ_BM, SGU_WIDTH), row_block),
        ),
        compiler_params=pltpu.CompilerParams(
            dimension_semantics=("arbitrary",),
            vmem_limit_bytes=VMEM_LIMIT),
        name="proj",
    )(h2, w_in)


GM_SUB = GM_BM // BLOCK
GATE_BN = 512
GM_GATE_CHUNKS = 2 * D_MODEL // GATE_BN
K_OFF = ATTN_WIDTH


def _mix_pieces(r, no_prev, sinks_ref, qkv_ref, u_ref, z_ref, bs_ref, lng_ref, lnb_ref,
                attn_ref, sgu_ref, kv_buf, ws_buf, bias_buf):
    rows = slice(r * BLOCK, (r + 1) * BLOCK)
    band = slice(r * BLOCK, (r + 2) * BLOCK)
    group_width = GQA_GROUP * HEAD_DIM

    def attention(hk):
        k_band = kv_buf[band, hk * HEAD_DIM:(hk + 1) * HEAD_DIM]
        v_band = kv_buf[band, KV_WIDTH + hk * HEAD_DIM:KV_WIDTH + (hk + 1) * HEAD_DIM]
        heads = range(hk * GQA_GROUP, (hk + 1) * GQA_GROUP)
        q = jnp.concatenate([qkv_ref[rows, h * HEAD_DIM:(h + 1) * HEAD_DIM] for h in heads], axis=0)
        s_all = lax.dot_general(q, k_band, (((1,), (1,)), ((), ())), preferred_element_type=F32)
        probs, denoms = [], []
        for gq, h in enumerate(heads):
            s = s_all[gq * BLOCK:(gq + 1) * BLOCK] + bias_buf[h]
            if no_prev is not None:
                s = s + no_prev
            sink = sinks_ref[h]
            m = jnp.maximum(jnp.max(s, axis=-1, keepdims=True), sink)
            p = jnp.exp(s - m)
            denoms.append(jnp.sum(p, axis=-1, keepdims=True) + jnp.exp(sink - m))
            probs.append(p.astype(BF16))
        o_all = jnp.dot(jnp.concatenate(probs, axis=0), v_band, preferred_element_type=F32)
        outs = [o_all[gq * BLOCK:(gq + 1) * BLOCK] / denoms[gq] for gq in range(GQA_GROUP)]
        attn_ref[rows, hk * group_width:(hk + 1) * group_width] = (
            jnp.concatenate(outs, axis=-1).astype(BF16))

    def gating():
        zc = z_ref[rows, :].astype(F32)
        mu = jnp.mean(zc, axis=-1, keepdims=True)
        zc = zc - mu
        var = jnp.mean(zc * zc, axis=-1, keepdims=True)
        zl = (zc * lax.rsqrt(var + EPS) * lng_ref[...] + lnb_ref[...]).astype(BF16)
        first_group = lax.broadcasted_iota(jnp.int32, (BLOCK, 2 * SGU_GROUP_DIM), 1) < SGU_GROUP_DIM
        zero = jnp.zeros((BLOCK, 2 * SGU_GROUP_DIM), BF16)
        mixed = []
        for j in range(SGU_GROUPS // 2):
            z_pair = zl[:, j * 2 * SGU_GROUP_DIM:(j + 1) * 2 * SGU_GROUP_DIM]
            z_diag = jnp.concatenate([jnp.where(first_group, z_pair, zero),
                                      jnp.where(first_group, zero, z_pair)], axis=0)
            mixed.append(jnp.dot(ws_buf[j], z_diag, preferred_element_type=F32))
        mixed = jnp.concatenate(mixed, axis=-1) + bs_ref[...]
        sgu_ref[rows, :] = (u_ref[rows, :].astype(F32) * mixed).astype(BF16)

    heads = [functools.partial(attention, hk) for hk in range(N_KV_HEADS)]
    return heads[:N_KV_HEADS // 2] + [gating] + heads[N_KV_HEADS // 2:]


def _gates_mixer_kernel(sinks_ref, h_ref, wg_ref, bg_ref, qkv_ref, halo_ref, u_ref, z_ref, ws_ref,
                        bs_ref, lng_ref, lnb_ref, gt_ref, attn_ref, sgu_ref,
                        kv_buf, ws_buf, bias_buf):
    i = pl.program_id(0)

    @pl.when(i == 0)
    def _():
        tq = lax.broadcasted_iota(jnp.int32, (BLOCK, BLOCK), 0)
        ts = lax.broadcasted_iota(jnp.int32, (BLOCK, BLOCK), 1)
        causal = ts <= tq
        for g in range(SGU_GROUPS):
            ws_buf[g // 2, :, (g % 2) * BLOCK:(g % 2 + 1) * BLOCK] = (
                jnp.where(causal, ws_ref[g], 0.0).astype(BF16))

        qpos = lax.broadcasted_iota(jnp.int32, (BLOCK, 2 * BLOCK), 0) + BLOCK
        kpos = lax.broadcasted_iota(jnp.int32, (BLOCK, 2 * BLOCK), 1)
        dist_i = qpos - kpos
        in_window = (dist_i >= 0) & (dist_i < BLOCK)
        dist = jnp.abs(dist_i).astype(F32)
        for h in range(N_Q_HEADS):
            alibi = -ALIBI_SLOPES[h] * dist
            bias_buf[h] = jnp.where(in_window, alibi, NEG_INF)

    kv_buf[0:BLOCK, :] = halo_ref[...]
    kv_buf[BLOCK:, :] = qkv_ref[:, K_OFF:]

    step_starts_sequence = (i % (SEQ // GM_BM)) == 0
    band_pos = lax.broadcasted_iota(jnp.int32, (BLOCK, 2 * BLOCK), 1)
    no_prev_mask = jnp.where(jnp.logical_and(step_starts_sequence, band_pos < BLOCK), NEG_INF, 0.0)

    blocks_per_part = GM_PART // BLOCK
    for part in range(GM_BM // GM_PART):
        pieces = []
        for r in range(part * blocks_per_part, (part + 1) * blocks_per_part):
            pieces += _mix_pieces(r, no_prev_mask if r == 0 else None,
                                  sinks_ref, qkv_ref, u_ref, z_ref, bs_ref, lng_ref, lnb_ref,
                                  attn_ref, sgu_ref, kv_buf, ws_buf, bias_buf)

        part_rows = slice(part * GM_PART, (part + 1) * GM_PART)
        h = h_ref[part_rows, :]
        for c in range(GM_GATE_CHUNKS):
            for piece in pieces[c * len(pieces) // GM_GATE_CHUNKS:(c + 1) * len(pieces) // GM_GATE_CHUNKS]:
                piece()
            cols = slice(c * GATE_BN, (c + 1) * GATE_BN)
            acc = jnp.dot(h, wg_ref[:, cols], preferred_element_type=F32)
            gt_ref[part_rows, cols] = jax.nn.sigmoid(acc + bg_ref[:, cols]).astype(BF16)


def _gates_mixer(h2, w_gate, b_gate, qkv, u, z, sinks, ws, bs_full, ln_g, ln_b):
    n = h2.shape[0]
    row_block = lambda i: (i, 0)
    const = lambda i: (0, 0)
    return pl.pallas_call(
        _gates_mixer_kernel,
        out_shape=(
            jax.ShapeDtypeStruct((n, 2 * D_MODEL), BF16),
            jax.ShapeDtypeStruct((n, ATTN_WIDTH), BF16),
            jax.ShapeDtypeStruct((n, SGU_WIDTH), BF16),
        ),
        grid=(n // GM_BM,),
        in_specs=[
            pl.BlockSpec(memory_space=pltpu.SMEM),
            pl.BlockSpec((GM_BM, D_MODEL), row_block),
            pl.BlockSpec((D_MODEL, 2 * D_MODEL), const, pipeline_mode=pl.Buffered(1)),
            pl.BlockSpec((1, 2 * D_MODEL), const),
            pl.BlockSpec((GM_BM, QKV_WIDTH), row_block),
            pl.BlockSpec((BLOCK, 2 * KV_WIDTH),
                         lambda i: (jnp.maximum(i * GM_SUB - 1, 0), ATTN_WIDTH // (2 * KV_WIDTH))),
            pl.BlockSpec((GM_BM, SGU_WIDTH), row_block),
            pl.BlockSpec((GM_BM, SGU_WIDTH), row_block),
            pl.BlockSpec((SGU_GROUPS, BLOCK, BLOCK), lambda i: (0, 0, 0)),
            pl.BlockSpec((BLOCK, SGU_WIDTH), const),
            pl.BlockSpec((1, SGU_WIDTH), const),
            pl.BlockSpec((1, SGU_WIDTH), const),
        ],
        out_specs=(
            pl.BlockSpec((GM_BM, 2 * D_MODEL), row_block),
            pl.BlockSpec((GM_BM, ATTN_WIDTH), row_block),
            pl.BlockSpec((GM_BM, SGU_WIDTH), row_block),
        ),
        scratch_shapes=[
            pltpu.VMEM((GM_BM + BLOCK, 2 * KV_WIDTH), BF16),
            pltpu.VMEM((SGU_GROUPS // 2, BLOCK, 2 * BLOCK), BF16),
            pltpu.VMEM((N_Q_HEADS, BLOCK, 2 * BLOCK), F32),
        ],
        compiler_params=pltpu.CompilerParams(
            dimension_semantics=("arbitrary",),
            vmem_limit_bytes=FFN_VMEM_LIMIT),
        name="gates_mixer",
    )(sinks, h2, w_gate, b_gate, qkv, qkv, u, z, ws, bs_full, ln_g, ln_b)


def _merge_kernel(attn_ref, sgu_ref, gt_ref, x_ref, pa_ref, pb_ref, wo_ref, o_ref, m_ref):
    attn = attn_ref[...]
    sgu = sgu_ref[...]
    for c in range(D_MODEL // MERGE_BN):
        cols = slice(c * MERGE_BN, (c + 1) * MERGE_BN)
        a = jnp.dot(attn, pa_ref[:, cols], preferred_element_type=F32)
        b = jnp.dot(sgu, pb_ref[:, cols], preferred_element_type=F32)
        g_a = gt_ref[:, cols].astype(F32)
        g_b = gt_ref[:, D_MODEL + c * MERGE_BN:D_MODEL + (c + 1) * MERGE_BN].astype(F32)
        m_ref[:, cols] = (g_a * a + g_b * b).astype(BF16)
    o_ref[...] = x_ref[...] + jnp.dot(m_ref[...], wo_ref[...], preferred_element_type=F32)


def _merge(attn, sgu, gates, x1, p_a, p_b, w_out):
    n = x1.shape[0]
    const = lambda i: (0, 0)
    return pl.pallas_call(
        _merge_kernel,
        out_shape=jax.ShapeDtypeStruct((n, D_MODEL), F32),
        grid=(n // MERGE_BM,),
        in_specs=[
            pl.BlockSpec((MERGE_BM, ATTN_WIDTH), lambda i: (i, 0)),
            pl.BlockSpec((MERGE_BM, SGU_WIDTH), lambda i: (i, 0)),
            pl.BlockSpec((MERGE_BM, 2 * D_MODEL), lambda i: (i, 0)),
            pl.BlockSpec((MERGE_BM, D_MODEL), lambda i: (i, 0)),
            pl.BlockSpec((ATTN_WIDTH, D_MODEL), const, pipeline_mode=pl.Buffered(1)),
            pl.BlockSpec((SGU_WIDTH, D_MODEL), const, pipeline_mode=pl.Buffered(1)),
            pl.BlockSpec((D_MODEL, D_MODEL), const, pipeline_mode=pl.Buffered(1)),
        ],
        out_specs=pl.BlockSpec((MERGE_BM, D_MODEL), lambda i: (i, 0)),
        scratch_shapes=[pltpu.VMEM((MERGE_BM, D_MODEL), BF16)],
        compiler_params=pltpu.CompilerParams(
            dimension_semantics=("arbitrary",),
            vmem_limit_bytes=VMEM_LIMIT),
        name="merge",
    )(attn, sgu, gates, x1, p_a, p_b, w_out)


def kernel(x, ffn1_norm, ffn1_w_gate, ffn1_w_up, ffn1_w_down, mix_norm, w_in, attn_sinks,
           sgu_norm_g, sgu_norm_b, sgu_w_s, sgu_b_s, w_proj_attn, w_proj_sgu,
           w_branch_gate, b_branch_gate, w_out, ffn2_norm, ffn2_w_gate, ffn2_w_up,
           ffn2_w_down, final_norm):
    batch, seq, d = x.shape
    assert (seq, d) == (SEQ, D_MODEL) and ffn1_norm.shape[0] == 1
    n = batch * seq
    xf = x.reshape(n, d)
    bf = lambda w: w.astype(BF16)
    row = lambda v: v.reshape(1, -1)

    later_weights = (w_in[0], w_branch_gate[0], w_proj_attn[0], w_proj_sgu[0], w_out[0],
                     ffn2_w_gate[0], ffn2_w_up[0], ffn2_w_down[0])
    (x1, h2, w_in_bf, w_bgate_bf, p_a_bf, p_b_bf, w_out_bf, f2_gate_bf, f2_up_bf, f2_down_bf) = _ffn(
        xf, row(ffn1_norm[0]), bf(ffn1_w_gate[0]), bf(ffn1_w_up[0]), bf(ffn1_w_down[0]),
        row(mix_norm[0]), final_norm=False, bf=FFN1_BF, casts=later_weights)

    qkv, u, z = _proj(h2, w_in_bf)

    bs_full = jnp.repeat(sgu_b_s[0].T, SGU_GROUP_DIM, axis=1)
    gates, attn, sgu = _gates_mixer(h2, w_bgate_bf, row(b_branch_gate[0]), qkv, u, z,
                                    attn_sinks[0], sgu_w_s[0], bs_full,
                                    row(sgu_norm_g[0]), row(sgu_norm_b[0]))

    x2 = _merge(attn, sgu, gates, x1, p_a_bf, p_b_bf, w_out_bf)

    (out,) = _ffn(x2, row(ffn2_norm[0]), f2_gate_bf, f2_up_bf, f2_down_bf,
                  row(final_norm), final_norm=True, bf=FFN2_BF)
    return out.reshape(batch, seq, d)
```
